```python
import functools
import jax, jax.numpy as jnp
from jax import lax
import numpy as np

D_MODEL = 2048
BATCH = 16
SEQ = 2048
DEPTH = 4
DEC_BATCH = 8
DEC_SEQ = 16
PAST_LEN = 4096

CHUNK = 64
D_MIX = D_MODEL
HEAD_DIM = 128
D_ATTN = D_MIX // 2
N_HEADS = D_ATTN // HEAD_DIM
D_CONV = D_MIX - D_ATTN
CONV_WIDTH = 31
CONV_BUF = CONV_WIDTH - 1
N_IN = 3 * D_ATTN + N_HEADS + 2 * D_CONV
Q_BLOCK = 128
ATTN_SCALE = HEAD_DIM ** -0.5
FORGET_BIAS_LO = 1.0
FORGET_BIAS_HI = 6.0
N_EXPERTS = 64
N_GROUPS = 8
TOPK_GROUPS = 4
TOPK = 8
D_EXPERT = 512
D_SHARED = 512
ROUTED_SCALE = 2.5
MOE_BLOCK = 128
ALPHA = (2 * DEPTH) ** 0.25
BETA = (8 * DEPTH) ** -0.25
LN_EPS = 1e-5

kernel_name = 'fox_conformer_moe_stream_step'


def layer_norm(x, g, b):
    xf = x.astype(jnp.float32)
    mu = jnp.mean(xf, -1, keepdims=True)
    var = jnp.mean(jnp.square(xf - mu), -1, keepdims=True)
    y = (xf - mu) * lax.rsqrt(var + LN_EPS)
    return (y * g.astype(jnp.float32) + b.astype(jnp.float32)).astype(x.dtype)


def adaln(c, w, b):
    m = jax.nn.silu(c) @ w + b
    return jnp.split(m[:, None, :], 6, axis=-1)


def modulate(x, shift, scale):
    return x * (1 + scale) + shift


def split_proj(u, w_in, b_f):
    p = u @ w_in
    lead = p.shape[:-1]
    q = p[..., 0:D_ATTN].reshape(*lead, N_HEADS, HEAD_DIM)
    k = p[..., D_ATTN:2 * D_ATTN].reshape(*lead, N_HEADS, HEAD_DIM)
    v = p[..., 2 * D_ATTN:3 * D_ATTN].reshape(*lead, N_HEADS, HEAD_DIM)
    o = 3 * D_ATTN
    logf = jax.nn.log_sigmoid((p[..., o:o + N_HEADS] + b_f).astype(jnp.float32))
    o = o + N_HEADS
    glu = p[..., o:o + D_CONV] * jax.nn.sigmoid(p[..., o + D_CONV:o + 2 * D_CONV])
    return q, k, v, logf, glu


def fox_attend(q, k, v, fq, fk, qpos, kpos):
    s = jnp.einsum('bqhd,bkhd->bhqk', q, k, preferred_element_type=jnp.float32) * ATTN_SCALE
    bias = jnp.transpose(fq, (0, 2, 1))[..., :, None] - jnp.transpose(fk, (0, 2, 1))[..., None, :]
    mask = kpos[None, :] <= qpos[:, None]
    p = jax.nn.softmax(jnp.where(mask, s + bias, -jnp.inf), axis=-1).astype(v.dtype)
    return jnp.einsum('bhqk,bkhd->bqhd', p, v)


def conv_module_tail(buf, conv_w, conv_b, g, b):
    y = lax.conv_general_dilated(buf, conv_w[:, None, :], (1,), 'VALID',
                                 dimension_numbers=('NWC', 'WIO', 'NWC'),
                                 feature_group_count=D_CONV)
    return jax.nn.silu(layer_norm(y + conv_b, g, b))


def mix_prompt(u, mix_w):
    w_in, b_f, conv_w, conv_b, cln_g, cln_b, w_out = mix_w
    B, S, _ = u.shape
    q, k, v, logf, glu = split_proj(u, w_in, b_f)
    F = jnp.cumsum(logf, axis=1)
    pos = jnp.arange(S)
    blocks = []
    for i in range(S // Q_BLOCK):
        lo, hi = i * Q_BLOCK, (i + 1) * Q_BLOCK
        blocks.append(fox_attend(q[:, lo:hi], k[:, :hi], v[:, :hi], F[:, lo:hi], F[:, :hi], pos[lo:hi], pos[:hi]))
    attn = jnp.concatenate(blocks, axis=1).reshape(B, S, D_ATTN)
    conv = conv_module_tail(jnp.pad(glu, ((0, 0), (CONV_BUF, 0), (0, 0))), conv_w, conv_b, cln_g, cln_b)
    out = jnp.concatenate([attn, conv], axis=-1) @ w_out
    return out, (k, v, logf, glu[:, S - CONV_BUF:])


def mix_sample(u, cache, mix_w):
    w_in, b_f, conv_w, conv_b, cln_g, cln_b, w_out = mix_w
    ck, cv, clf, cconv = cache
    B, n, _ = u.shape
    P = ck.shape[1]
    q, k, v, logf, glu = split_proj(u, w_in, b_f)
    k_all = jnp.concatenate([ck, k], axis=1)
    v_all = jnp.concatenate([cv, v], axis=1)
    F_all = jnp.cumsum(jnp.concatenate([clf.astype(jnp.float32), logf], axis=1), axis=1)
    pos = jnp.arange(P + n)
    attn = fox_attend(q, k_all, v_all, F_all[:, P:], F_all, pos[P:], pos).reshape(B, n, D_ATTN)
    buf = jnp.concatenate([cconv, glu], axis=1)
    conv = conv_module_tail(buf, conv_w, conv_b, cln_g, cln_b)
    out = jnp.concatenate([attn, conv], axis=-1) @ w_out
    return out, (k, v, logf, buf[:, n:])


def route(h, w_router, e_bias):
    T = h.shape[0]
    s = jax.nn.sigmoid((h @ w_router).astype(jnp.float32))
    sb = s + e_bias.astype(jnp.float32)
    grp = sb.reshape(T, N_GROUPS, N_EXPERTS // N_GROUPS)
    gscore = lax.top_k(grp, 2)[0].sum(-1)
    _, gidx = lax.top_k(gscore, TOPK_GROUPS)
    gmask = jax.nn.one_hot(gidx, N_GROUPS, dtype=jnp.float32).sum(1) > 0
    emask = jnp.repeat(gmask, N_EXPERTS // N_GROUPS, axis=1)
    _, eidx = lax.top_k(jnp.where(emask, sb, -jnp.inf), TOPK)
    w = jnp.take_along_axis(s, eidx, axis=-1)
    w = w / jnp.sum(w, -1, keepdims=True) * ROUTED_SCALE
    return eidx.astype(jnp.int32), w


def routed_experts(h, eidx, gw, w_e_gate, w_e_up, w_e_down):
    T = h.shape[0]
    A = T * TOPK
    NB = -(-A // MOE_BLOCK) + N_EXPERTS
    flat_e = eidx.reshape(-1)
    flat_tok = jnp.repeat(jnp.arange(T, dtype=jnp.int32), TOPK)
    flat_w = gw.reshape(-1)
    order = jnp.argsort(flat_e)
    se, stok, sw = flat_e[order], flat_tok[order], flat_w[order]
    counts = jnp.bincount(flat_e, length=N_EXPERTS).astype(jnp.int32)
    padded = (counts + MOE_BLOCK - 1) // MOE_BLOCK * MOE_BLOCK
    start = jnp.cumsum(counts) - counts
    pend = jnp.cumsum(padded)
    pstart = pend - padded
    slot = pstart[se] + (jnp.arange(A, dtype=jnp.int32) - start[se])
    slot_tok = jnp.full((NB * MOE_BLOCK,), T, dtype=jnp.int32).at[slot].set(stok)
    slot_w = jnp.zeros((NB * MOE_BLOCK,), h.dtype).at[slot].set(sw)
    block_e = jnp.minimum(jnp.searchsorted(pend, jnp.arange(NB, dtype=jnp.int32) * MOE_BLOCK, side='right'),
                          N_EXPERTS - 1)
    h_pad = jnp.concatenate([h, jnp.zeros((1, h.shape[1]), h.dtype)], axis=0)

    def body(acc, xs):
        tok, w, e = xs
        xb = h_pad[tok]
        y = (jax.nn.silu(xb @ w_e_gate[e]) * (xb @ w_e_up[e])) @ w_e_down[e]
        return acc.at[tok].add(y * w[:, None]), None

    acc0 = jnp.zeros((T + 1, h.shape[1]), h.dtype)
    acc, _ = lax.scan(body, acc0, (slot_tok.reshape(NB, MOE_BLOCK), slot_w.reshape(NB, MOE_BLOCK), block_e))
    return acc[:T]


def moe(h3, moe_p):
    w_router, e_bias, w_e_gate, w_e_up, w_e_down, w_s_gate, w_s_up, w_s_down = moe_p
    lead = h3.shape[:-1]
    h = h3.reshape(-1, D_MODEL)
    shared = (jax.nn.silu(h @ w_s_gate) * (h @ w_s_up)) @ w_s_down
    eidx, gw = route(h, w_router, e_bias)
    routed = routed_experts(h, eidx, gw.astype(h.dtype), w_e_gate, w_e_up, w_e_down)
    return (shared + routed).reshape(*lead, D_MODEL)


def trunk_layer(x, c, mix_fn, norm_p, moe_p):
    w_ada_l, b_ada_l, ln1_g_l, ln1_b_l, ln2_g_l, ln2_b_l = norm_p
    sh1, sc1, g1, sh2, sc2, g2 = adaln(c, w_ada_l, b_ada_l)
    mix, state = mix_fn(modulate(x, sh1, sc1))
    x = layer_norm(ALPHA * x + (1 + g1) * mix, ln1_g_l, ln1_b_l)
    x = layer_norm(ALPHA * x + (1 + g2) * moe(modulate(x, sh2, sc2), moe_p), ln2_g_l, ln2_b_l)
    return x, state


def setup_inputs(seed: int = 0) -> dict:
    key = jax.random.key(seed)
    keys = jax.random.split(key, 32)

    def nrm(i, shape, scale=1.0):
        return jax.random.normal(keys[i], shape, jnp.float32) * scale

    L, D = DEPTH, D_MODEL
    head_bias = jnp.linspace(FORGET_BIAS_LO, FORGET_BIAS_HI, N_HEADS)
    return {
        'x_prompt': nrm(0, (BATCH, SEQ, D)),
        'x_sample': nrm(1, (DEC_BATCH, DEC_SEQ, D)),
        'c_prompt': nrm(2, (BATCH, D)),
        'c_sample': nrm(3, (DEC_BATCH, D)),
        'cache_k': nrm(4, (L, DEC_BATCH, PAST_LEN, N_HEADS, HEAD_DIM)),
        'cache_v': nrm(5, (L, DEC_BATCH, PAST_LEN, N_HEADS, HEAD_DIM)),
        'cache_logf': jax.nn.log_sigmoid(head_bias + nrm(6, (L, DEC_BATCH, PAST_LEN, N_HEADS))),
        'state_conv': nrm(7, (L, DEC_BATCH, CONV_BUF, D_CONV), 0.5),
        'w_ada': nrm(8, (L, D, 6 * D), 0.1 * D ** -0.5),
        'b_ada': nrm(9, (L, 6 * D), 0.02),
        'w_in': nrm(10, (L, D, N_IN), D ** -0.5),
        'b_f': head_bias[None, :] + nrm(11, (L, N_HEADS), 0.1),
        'conv_w': nrm(12, (L, CONV_WIDTH, D_CONV), CONV_WIDTH ** -0.5),
        'conv_b': nrm(13, (L, D_CONV), 0.02),
        'conv_ln_g': 1.0 + nrm(14, (L, D_CONV), 0.02),
        'conv_ln_b': nrm(15, (L, D_CONV), 0.02),
        'w_out': nrm(16, (L, D_MIX, D), BETA * D_MIX ** -0.5),
        'ln1_g': 1.0 + nrm(17, (L, D), 0.02),
        'ln1_b': nrm(18, (L, D), 0.02),
        'w_router': nrm(19, (L, D, N_EXPERTS), D ** -0.5),
        'e_bias': nrm(20, (L, N_EXPERTS), 0.01),
        'w_e_gate': nrm(21, (L, N_EXPERTS, D, D_EXPERT), D ** -0.5),
        'w_e_up': nrm(22, (L, N_EXPERTS, D, D_EXPERT), D ** -0.5),
        'w_e_down': nrm(23, (L, N_EXPERTS, D_EXPERT, D), BETA * D_EXPERT ** -0.5),
        'w_s_gate': nrm(24, (L, D, D_SHARED), D ** -0.5),
        'w_s_up': nrm(25, (L, D, D_SHARED), D ** -0.5),
        'w_s_down': nrm(26, (L, D_SHARED, D), BETA * D_SHARED ** -0.5),
        'ln2_g': 1.0 + nrm(27, (L, D), 0.02),
        'ln2_b': nrm(28, (L, D), 0.02),
    }


def reference(x_prompt, x_sample, c_prompt, c_sample, cache_k, cache_v, cache_logf, state_conv,
              w_ada, b_ada, w_in, b_f, conv_w, conv_b, conv_ln_g, conv_ln_b, w_out, ln1_g, ln1_b,
              w_router, e_bias, w_e_gate, w_e_up, w_e_down, w_s_gate, w_s_up, w_s_down, ln2_g, ln2_b):
    xp, xs = x_prompt, x_sample
    kp, vp, lfp, cvp = [], [], [], []
    ks, vs, lfs, cvs = [], [], [], []
    for l in range(DEPTH):
        mix_w = (w_in[l], b_f[l], conv_w[l], conv_b[l], conv_ln_g[l], conv_ln_b[l], w_out[l])
        norm_p = (w_ada[l], b_ada[l], ln1_g[l], ln1_b[l], ln2_g[l], ln2_b[l])
        moe_p = (w_router[l], e_bias[l], w_e_gate[l], w_e_up[l], w_e_down[l], w_s_gate[l], w_s_up[l], w_s_down[l])
        xp, st = trunk_layer(xp, c_prompt, functools.partial(mix_prompt, mix_w=mix_w), norm_p, moe_p)
        kp.append(st[0]); vp.append(st[1]); lfp.append(st[2]); cvp.append(st[3])
        cache_l = (cache_k[l], cache_v[l], cache_logf[l], state_conv[l])
        xs, st = trunk_layer(xs, c_sample, functools.partial(mix_sample, cache=cache_l, mix_w=mix_w), norm_p, moe_p)
        ks.append(st[0]); vs.append(st[1]); lfs.append(st[2]); cvs.append(st[3])
    return (xp, xs, jnp.stack(kp), jnp.stack(vp), jnp.stack(lfp), jnp.stack(cvp),
            jnp.stack(ks), jnp.stack(vs), jnp.stack(lfs), jnp.stack(cvs))
```

```python
import functools

import jax
import jax.numpy as jnp
from jax import lax
from jax.experimental import pallas as pl
from jax.experimental.pallas import tpu as pltpu

F32 = jnp.float32
BF16 = jnp.bfloat16
I32 = jnp.int32

HEAD_DIM = 128
N_GROUPS = 8
TOPK_GROUPS = 4
TOPK = 8
ROUTED_SCALE = 2.5
LN_EPS = 1e-5
NEG_BIG = -1e30

CONV_HALO = 32
ROUTER_LANES = 128
VMEM_LIMIT = 56 * 1024 * 1024


def _cparams(*sem):
    return pltpu.CompilerParams(dimension_semantics=sem, vmem_limit_bytes=VMEM_LIMIT)


def _tile(n, target, mult=8):
    if n <= target:
        return n
    t = (target // mult) * mult
    while t >= mult:
        if n % t == 0:
            return t
        t -= mult
    return n


def _sigmoid(x):
    return 1.0 / (1.0 + jnp.exp(-x))


def _log_sigmoid(x):
    return jnp.minimum(x, 0.0) - jnp.log(1.0 + jnp.exp(-jnp.abs(x)))


def _layer_norm(y, g, b):
    mu = jnp.mean(y, axis=-1, keepdims=True)
    d = y - mu
    var = jnp.mean(d * d, axis=-1, keepdims=True)
    return d * lax.rsqrt(var + LN_EPS) * g + b


def _per_batch(fn, x, refs, bt, rows):
    if bt == 1:
        return fn(x, *[r[0] for r in refs])
    tm, d = x.shape
    out = fn(x.reshape(bt, rows, d), *[r[...] for r in refs])
    return out.reshape(tm, d)


def _batch_tiling(n_rows, rows_per_batch, target):
    if rows_per_batch >= target:
        tm = _tile(rows_per_batch, target)
        per = rows_per_batch // tm
        return tm, 1, (lambda i: i // per)
    nb = n_rows // rows_per_batch
    bt = _tile(nb, max(1, target // rows_per_batch), 1)
    return bt * rows_per_batch, bt, (lambda i: i)


def _adaln_kernel(c_ref, w_ref, b_ref, o_ref):
    c = c_ref[...]
    a = (c * _sigmoid(c)).astype(BF16)
    o_ref[0] = jnp.dot(a, w_ref[0].astype(BF16), preferred_element_type=F32) + b_ref[0]


def _adaln(c_all, w_ada, b_ada):
    depth, d, n6 = w_ada.shape
    bc = c_all.shape[0]
    tn = _tile(n6, 1024, 128)
    return pl.pallas_call(
        _adaln_kernel,
        grid=(depth, n6 // tn),
        in_specs=[pl.BlockSpec((bc, d), lambda l, j: (0, 0)),
                  pl.BlockSpec((1, d, tn), lambda l, j: (l, 0, j)),
                  pl.BlockSpec((1, 1, tn), lambda l, j: (l, 0, j))],
        out_specs=pl.BlockSpec((1, bc, tn), lambda l, j: (l, 0, j)),
        out_shape=jax.ShapeDtypeStruct((depth, bc, n6), F32),
        compiler_params=_cparams("arbitrary", "arbitrary"),
    )(c_all, w_ada, b_ada.reshape(depth, 1, n6))


def _in_proj_kernel(x_ref, sh_ref, sc_ref, w_ref, wf_ref, p_ref, f_ref, u_ref, *, bt, rows):
    @pl.when(pl.program_id(1) == 0)
    def _():
        u = _per_batch(lambda x, sh, sc: x * (1.0 + sc) + sh, x_ref[...], (sh_ref, sc_ref), bt, rows)
        ub = u.astype(BF16)
        u_ref[...] = ub
        f_ref[...] = jnp.dot(ub, wf_ref[...], preferred_element_type=F32)

    p_ref[...] = jnp.dot(u_ref[...], w_ref[...], preferred_element_type=F32)


def _in_proj(x2, shift, scale, w_main, w_f, rows_per_batch):
    t, d = x2.shape
    n_main = w_main.shape[1]
    h = w_f.shape[1]
    tm, bt, vec_idx = _batch_tiling(t, rows_per_batch, 1024)
    tn = _tile(n_main, 512, 128)
    vec_spec = pl.BlockSpec((bt, 1, d), lambda i, j: (vec_idx(i), 0, 0))
    return pl.pallas_call(
        functools.partial(_in_proj_kernel, bt=bt, rows=rows_per_batch),
        grid=(t // tm, n_main // tn),
        in_specs=[pl.BlockSpec((tm, d), lambda i, j: (i, 0)), vec_spec, vec_spec,
                  pl.BlockSpec((d, tn), lambda i, j: (0, j)),
                  pl.BlockSpec((d, h), lambda i, j: (0, 0))],
        out_specs=[pl.BlockSpec((tm, tn), lambda i, j: (i, j)),
                   pl.BlockSpec((tm, h), lambda i, j: (i, 0))],
        out_shape=[jax.ShapeDtypeStruct((t, n_main), F32), jax.ShapeDtypeStruct((t, h), F32)],
        scratch_shapes=[pltpu.VMEM((tm, d), BF16)],
        compiler_params=_cparams("arbitrary", "arbitrary"),
    )(x2, shift, scale, w_main, w_f)


LOG2E = 1.4426950408889634
AUG_TERMS = 3


def _bias_placement(n_heads):
    rows = (AUG_TERMS + 1) * n_heads
    eq = [[0.0] * (n_heads * HEAD_DIM) for _ in range(rows)]
    ek = [[0.0] * (n_heads * HEAD_DIM) for _ in range(rows)]
    for h in range(n_heads):
        for n in range(AUG_TERMS):
            eq[n * n_heads + h][h * HEAD_DIM + n] = 1.0
            ek[n * n_heads + h][h * HEAD_DIM + AUG_TERMS + n] = -1.0
            eq[AUG_TERMS * n_heads + h][h * HEAD_DIM + AUG_TERMS + n] = 1.0
            ek[AUG_TERMS * n_heads + h][h * HEAD_DIM + n] = 1.0
    return jnp.array(eq, BF16), jnp.array(ek, BF16)


def _cumsum_kernel(z_ref, bf_ref, c0_ref, eq_ref, ek_ref, lf_ref, cum_ref, aq_ref, ak_ref, *, logsig, chunk):
    s, n_heads = z_ref.shape[1], z_ref.shape[2]
    r = lax.broadcasted_iota(I32, (chunk, chunk), 0)
    c = lax.broadcasted_iota(I32, (chunk, chunk), 1)
    tri = (r >= c).astype(F32)
    carry = c0_ref[0]
    for ci in range(s // chunk):
        sl = slice(ci * chunk, (ci + 1) * chunk)
        z = z_ref[0, sl, :]
        lf = _log_sigmoid(z + bf_ref[...]) if logsig else z
        lf_ref[0, sl, :] = lf
        cum = jnp.dot(tri, lf, precision=lax.Precision.HIGHEST, preferred_element_type=F32) + carry
        cum_ref[0, sl, :] = cum
        carry = cum[chunk - 1:chunk, :]
        rem = cum * LOG2E
        pieces = []
        for _ in range(AUG_TERMS):
            piece = rem.astype(BF16).astype(F32)
            pieces.append(piece)
            rem = rem - piece
        pm = jnp.concatenate(pieces + [jnp.ones((chunk, n_heads), F32)], axis=1).astype(BF16)
        aq_ref[0, sl, :] = jnp.dot(pm, eq_ref[...], preferred_element_type=F32).astype(BF16)
        ak_ref[0, sl, :] = jnp.dot(pm, ek_ref[...], preferred_element_type=F32).astype(BF16)


def _forget_cumsum(z, b_f_row, carry0, logsig):
    b, s, h = z.shape
    chunk = _tile(s, 256)
    blk = pl.BlockSpec((1, s, h), lambda i: (i, 0, 0))
    aug = pl.BlockSpec((1, s, h * HEAD_DIM), lambda i: (i, 0, 0))
    eq, ek = _bias_placement(h)
    place = pl.BlockSpec(eq.shape, lambda i: (0, 0))
    return pl.pallas_call(
        functools.partial(_cumsum_kernel, logsig=logsig, chunk=chunk),
        grid=(b,),
        in_specs=[blk, pl.BlockSpec((1, h), lambda i: (0, 0)), pl.BlockSpec((1, 1, h), lambda i: (i, 0, 0)),
                  place, place],
        out_specs=[blk, blk, aug, aug],
        out_shape=[jax.ShapeDtypeStruct((b, s, h), F32)] * 2
        + [jax.ShapeDtypeStruct((b, s, h * HEAD_DIM), BF16)] * 2,
        compiler_params=_cparams("arbitrary"),
    )(z, b_f_row, carry0, eq, ek)


def _attn_kernel(q_ref, k_ref, v_ref, aq_ref, ak_ref, o_ref, m_ref, acc_ref, *, n_heads, tq, tk, q_off, scale):
    qi = pl.program_id(1)
    ki = pl.program_id(2)

    @pl.when(ki == 0)
    def _():
        m_ref[...] = jnp.full(m_ref.shape, NEG_BIG, F32)
        acc_ref[...] = jnp.zeros(acc_ref.shape, F32)

    q_start = qi * tq + q_off
    k_start = ki * tk

    @pl.when(k_start <= q_start + (tq - 1))
    def _():
        qpos = q_start + lax.broadcasted_iota(I32, (tq, tk), 0)
        kpos = k_start + lax.broadcasted_iota(I32, (tq, tk), 1)
        visible = kpos <= qpos
        ones = jnp.ones((tk, HEAD_DIM), BF16)
        for h in range(n_heads):
            hs = slice(h * HEAD_DIM, (h + 1) * HEAD_DIM)
            qa = jnp.concatenate([(q_ref[0, :, hs] * (scale * LOG2E)).astype(BF16), aq_ref[0, :, hs]], axis=1)
            ka = jnp.concatenate([k_ref[0, :, hs].astype(BF16), ak_ref[0, :, hs]], axis=1)
            va = jnp.concatenate([v_ref[0, :, hs].astype(BF16), ones], axis=1)
            s = lax.dot_general(qa, ka, (((1,), (1,)), ((), ())), preferred_element_type=F32)
            s = jnp.where(visible, s, NEG_BIG)
            m_prev = m_ref[h]
            m_new = jnp.maximum(m_prev, jnp.max(s, axis=-1, keepdims=True))
            p = jnp.exp2(s - m_new).astype(BF16)
            acc_ref[h] = jnp.exp2(m_prev - m_new) * acc_ref[h] + jnp.dot(p, va, preferred_element_type=F32)
            m_ref[h] = m_new

    @pl.when(ki == pl.num_programs(2) - 1)
    def _():
        for h in range(n_heads):
            acc = acc_ref[h]
            o_ref[0, :, h * HEAD_DIM:(h + 1) * HEAD_DIM] = (
                acc[:, :HEAD_DIM] / acc[:, HEAD_DIM:HEAD_DIM + 1]).astype(o_ref.dtype)


def _attention(q_arr, q_col, k_arr, k_col, v_arr, v_col, aug_q, aug_k, q_off):
    b, sq, da = aug_q.shape
    sk = aug_k.shape[1]
    n_heads = da // HEAD_DIM
    tq = _tile(sq, 512)
    tk = _tile(sk, 512)

    def last_k(i):
        return (i * tq + q_off + tq - 1) // tk

    kern = functools.partial(_attn_kernel, n_heads=n_heads, tq=tq, tk=tk, q_off=q_off, scale=HEAD_DIM ** -0.5)
    return pl.pallas_call(
        kern,
        grid=(b, sq // tq, sk // tk),
        in_specs=[pl.BlockSpec((1, tq, da), lambda bi, i, j: (bi, i, q_col)),
                  pl.BlockSpec((1, tk, da), lambda bi, i, j: (bi, jnp.minimum(j, last_k(i)), k_col)),
                  pl.BlockSpec((1, tk, da), lambda bi, i, j: (bi, jnp.minimum(j, last_k(i)), v_col)),
                  pl.BlockSpec((1, tq, da), lambda bi, i, j: (bi, i, 0)),
                  pl.BlockSpec((1, tk, da), lambda bi, i, j: (bi, jnp.minimum(j, last_k(i)), 0))],
        out_specs=pl.BlockSpec((1, tq, da), lambda bi, i, j: (bi, i, 0)),
        out_shape=jax.ShapeDtypeStruct((b, sq, da), BF16),
        scratch_shapes=[pltpu.VMEM((n_heads, tq, 1), F32), pltpu.VMEM((n_heads, tq, 2 * HEAD_DIM), F32)],
        compiler_params=_cparams("arbitrary", "arbitrary", "arbitrary"),
    )(q_arr, k_arr, v_arr, aug_q, aug_k)


SUBLANES = 8


def _conv_kernel(a_ref, b_ref, ha_ref, hb_ref, w_ref, cb_ref, g_ref, be_ref, o_ref, tail_ref, xbuf, xsh, ybuf,
                 *, tt, kw, halo_is_glu, zero_first):
    glu = a_ref[0] * _sigmoid(b_ref[0])
    xbuf[CONV_HALO:CONV_HALO + tt, :] = glu
    halo = ha_ref[0] if halo_is_glu else ha_ref[0] * _sigmoid(hb_ref[0])
    if zero_first:
        halo = jnp.where(pl.program_id(1) == 0, 0.0, halo)
    xbuf[0:CONV_HALO, :] = halo
    tail_ref[0] = xbuf[tt:tt + CONV_HALO, :]

    span = tt + CONV_HALO - SUBLANES
    for s in range(1, SUBLANES):
        xsh[s - 1, 0:span, :] = xbuf[s:s + span, :]

    channels = xbuf.shape[1]
    rc = _tile(tt, 64)
    first = CONV_HALO - (kw - 1)
    for c0 in range(0, channels, 128):
        cs = slice(c0, c0 + 128)
        for r0 in range(0, tt, rc):
            acc = jnp.zeros((rc, 128), F32)
            for j in range(kw):
                shift = (first + j) % SUBLANES
                base = (first + j) - shift + r0
                src = xbuf if shift == 0 else xsh.at[shift - 1]
                acc = acc + src[base:base + rc, cs] * w_ref[j:j + 1, cs]
            ybuf[r0:r0 + rc, cs] = acc
    y = _layer_norm(ybuf[...] + cb_ref[...], g_ref[...], be_ref[...])
    o_ref[0] = (y * _sigmoid(y)).astype(o_ref.dtype)


def _conv_module(p3, a_col, halo_arr, halo_a_col, halo_b_col, halo_is_glu, conv_w, conv_b, ln_g, ln_b):
    b, s, _ = p3.shape
    kw, c = conv_w.shape
    assert kw - 1 <= CONV_HALO
    tt = _tile(s, 128)
    per = tt // CONV_HALO if tt >= CONV_HALO else 1

    def halo_row(i):
        return jnp.maximum(i * per - 1, 0)

    kern = functools.partial(_conv_kernel, tt=tt, kw=kw, halo_is_glu=halo_is_glu, zero_first=not halo_is_glu)
    vec = pl.BlockSpec((1, c), lambda bi, i: (0, 0))
    return pl.pallas_call(
        kern,
        grid=(b, s // tt),
        in_specs=[pl.BlockSpec((1, tt, c), lambda bi, i: (bi, i, a_col)),
                  pl.BlockSpec((1, tt, c), lambda bi, i: (bi, i, a_col + 1)),
                  pl.BlockSpec((1, CONV_HALO, c), lambda bi, i: (bi, halo_row(i), halo_a_col)),
                  pl.BlockSpec((1, CONV_HALO, c), lambda bi, i: (bi, halo_row(i), halo_b_col)),
                  pl.BlockSpec((kw, c), lambda bi, i: (0, 0)), vec, vec, vec],
        out_specs=[pl.BlockSpec((1, tt, c), lambda bi, i: (bi, i, 0)),
                   pl.BlockSpec((1, CONV_HALO, c), lambda bi, i: (bi, 0, 0))],
        out_shape=[jax.ShapeDtypeStruct((b, s, c), BF16), jax.ShapeDtypeStruct((b, CONV_HALO, c), F32)],
        scratch_shapes=[pltpu.VMEM((CONV_HALO + tt, c), F32), pltpu.VMEM((SUBLANES - 1, CONV_HALO + tt, c), F32),
                        pltpu.VMEM((tt, c), F32)],
        compiler_params=_cparams("arbitrary", "arbitrary"),
    )(p3, p3, halo_arr, halo_arr, conv_w, conv_b, ln_g, ln_b)


def _out_proj_kernel(at_ref, cv_ref, x_ref, g1_ref, sh2_ref, sc2_ref, wa_ref, wc_ref, lg_ref, lb_ref,
                     wrh_ref, wrl_ref, x1_ref, h_ref, lo_ref, *, bt, rows, alpha):
    mix = (jnp.dot(at_ref[...], wa_ref[...], preferred_element_type=F32)
           + jnp.dot(cv_ref[...], wc_ref[...], preferred_element_type=F32))
    y = alpha * x_ref[...] + _per_batch(lambda m, g: (1.0 + g) * m, mix, (g1_ref,), bt, rows)
    x1 = _layer_norm(y, lg_ref[...], lb_ref[...])
    x1_ref[...] = x1
    h = _per_batch(lambda x, sh, sc: x * (1.0 + sc) + sh, x1, (sh2_ref, sc2_ref), bt, rows)
    h_ref[...] = h
    hh = h.astype(BF16)
    hl = (h - hh.astype(F32)).astype(BF16)
    lo_ref[...] = (jnp.dot(hh, wrh_ref[...], preferred_element_type=F32)
                   + jnp.dot(hh, wrl_ref[...], preferred_element_type=F32)
                   + jnp.dot(hl, wrh_ref[...], preferred_element_type=F32))


def _out_proj(attn2, conv2, x2, gate1, shift2, scale2, w_out_a, w_out_c, ln_g, ln_b, wr_hi, wr_lo,
              rows_per_batch, alpha):
    t, d = x2.shape
    da = attn2.shape[1]
    dc = conv2.shape[1]
    tm, bt, vec_idx = _batch_tiling(t, rows_per_batch, 256)
    vec_spec = pl.BlockSpec((bt, 1, d), lambda i: (vec_idx(i), 0, 0))
    row = pl.BlockSpec((1, d), lambda i: (0, 0))
    full = lambda a: pl.BlockSpec(a.shape, lambda i: (0, 0))
    return pl.pallas_call(
        functools.partial(_out_proj_kernel, bt=bt, rows=rows_per_batch, alpha=alpha),
        grid=(t // tm,),
        in_specs=[pl.BlockSpec((tm, da), lambda i: (i, 0)), pl.BlockSpec((tm, dc), lambda i: (i, 0)),
                  pl.BlockSpec((tm, d), lambda i: (i, 0)), vec_spec, vec_spec, vec_spec,
                  full(w_out_a), full(w_out_c), row, row, full(wr_hi), full(wr_lo)],
        out_specs=[pl.BlockSpec((tm, d), lambda i: (i, 0)), pl.BlockSpec((tm, d), lambda i: (i, 0)),
                   pl.BlockSpec((tm, ROUTER_LANES), lambda i: (i, 0))],
        out_shape=[jax.ShapeDtypeStruct((t, d), F32), jax.ShapeDtypeStruct((t, d), F32),
                   jax.ShapeDtypeStruct((t, ROUTER_LANES), F32)],
        compiler_params=_cparams("arbitrary"),
    )(attn2, conv2, x2, gate1, shift2, scale2, w_out_a, w_out_c, ln_g, ln_b, wr_hi, wr_lo)


def _route_kernel(lo_ref, eb_ref, eidx_ref, gw_ref, rank_ref, cnt_ref, carry_ref, *, n_experts, tr):
    i = pl.program_id(0)
    gsz = n_experts // N_GROUPS

    @pl.when(i == 0)
    def _():
        carry_ref[...] = jnp.zeros(carry_ref.shape, F32)

    logits = lo_ref[...].T[0:n_experts, :]
    sub = lax.broadcasted_iota(I32, (gsz, tr), 0)
    s_g, sb_g, gid_g = [], [], []
    for g in range(N_GROUPS):
        rs = slice(g * gsz, (g + 1) * gsz)
        s = _sigmoid(logits[rs, :])
        s_g.append(s)
        sb_g.append(s + eb_ref[rs, :])
        gid_g.append(sub + g * gsz)

    gscore = []
    for g in range(N_GROUPS):
        x = sb_g[g]
        m1 = jnp.max(x, axis=0, keepdims=True)
        i1 = jnp.min(jnp.where(x == m1, sub, gsz), axis=0, keepdims=True)
        m2 = jnp.max(jnp.where(sub == i1, -jnp.inf, x), axis=0, keepdims=True)
        gscore.append(m1 + m2)
    gs = jnp.concatenate(gscore, axis=0)
    gi = lax.broadcasted_iota(I32, (N_GROUPS, tr), 0)
    sel = jnp.zeros((N_GROUPS, tr), F32)
    for _ in range(TOPK_GROUPS):
        gm = jnp.max(gs, axis=0, keepdims=True)
        pick = gi == jnp.min(jnp.where(gs == gm, gi, N_GROUPS), axis=0, keepdims=True)
        sel = jnp.where(pick, 1.0, sel)
        gs = jnp.where(pick, -jnp.inf, gs)

    cand = [jnp.where(sel[g:g + 1, :] > 0.0, sb_g[g], -jnp.inf) for g in range(N_GROUPS)]
    member = [jnp.zeros((gsz, tr), F32) for _ in range(N_GROUPS)]
    eidx, wsel = [], []
    for _ in range(TOPK):
        mx = cand[0]
        for g in range(1, N_GROUPS):
            mx = jnp.maximum(mx, cand[g])
        mx = jnp.max(mx, axis=0, keepdims=True)
        ei = jnp.where(cand[0] == mx, gid_g[0], n_experts)
        for g in range(1, N_GROUPS):
            ei = jnp.minimum(ei, jnp.where(cand[g] == mx, gid_g[g], n_experts))
        ei = jnp.min(ei, axis=0, keepdims=True)
        w = jnp.zeros((gsz, tr), F32)
        for g in range(N_GROUPS):
            hit = gid_g[g] == ei
            w = w + jnp.where(hit, s_g[g], 0.0)
            member[g] = jnp.where(hit, 1.0, member[g])
            cand[g] = jnp.where(hit, -jnp.inf, cand[g])
        eidx.append(ei)
        wsel.append(jnp.sum(w, axis=0, keepdims=True))
    w8 = jnp.concatenate(wsel, axis=0)
    gw_ref[...] = w8 / jnp.sum(w8, axis=0, keepdims=True) * ROUTED_SCALE
    eidx_ref[...] = jnp.concatenate(eidx, axis=0)

    mt = jnp.concatenate(member, axis=0)
    r = lax.broadcasted_iota(I32, (tr, tr), 0)
    c = lax.broadcasted_iota(I32, (tr, tr), 1)
    before = (r < c).astype(BF16)
    prior = jnp.dot(mt.astype(BF16), before, preferred_element_type=F32) + carry_ref[...]
    eio = lax.broadcasted_iota(I32, (n_experts, tr), 0)
    ranks = [jnp.sum(jnp.where(eio == ei, prior, 0.0), axis=0, keepdims=True) for ei in eidx]
    rank_ref[...] = jnp.concatenate(ranks, axis=0).astype(I32)
    carry_ref[...] = carry_ref[...] + jnp.sum(mt, axis=1, keepdims=True)
    cnt_ref[...] = carry_ref[...]


def _route(logits, e_bias_col):
    t = logits.shape[0]
    n_experts = e_bias_col.shape[0]
    tr = _tile(t, 512, 128)
    out_blk = pl.BlockSpec((TOPK, tr), lambda i: (0, i))
    return pl.pallas_call(
        functools.partial(_route_kernel, n_experts=n_experts, tr=tr),
        grid=(t // tr,),
        in_specs=[pl.BlockSpec((tr, ROUTER_LANES), lambda i: (i, 0)),
                  pl.BlockSpec((n_experts, 1), lambda i: (0, 0))],
        out_specs=[out_blk, out_blk, out_blk, pl.BlockSpec((n_experts, 1), lambda i: (0, 0))],
        out_shape=[jax.ShapeDtypeStruct((TOPK, t), I32), jax.ShapeDtypeStruct((TOPK, t), F32),
                   jax.ShapeDtypeStruct((TOPK, t), I32), jax.ShapeDtypeStruct((n_experts, 1), F32)],
        scratch_shapes=[pltpu.VMEM((n_experts, 1), F32)],
        compiler_params=_cparams("arbitrary"),
    )(logits, e_bias_col)


def _row_copy(src, src_row, dst, dst_row, sem):
    return pltpu.make_async_copy(src.at[pl.ds(src_row, 1), :], dst.at[pl.ds(dst_row, 1), :], sem)


def _experts_kernel(be_ref, tokc_ref, tokn_ref, dstp_ref, dstc_ref, h_hbm, wg_ref, wu_ref, wd_ref, y_hbm,
                    xbuf0, xbuf1, ybuf0, ybuf1, gsem, ssem, *, tm):
    i = pl.program_id(0)
    nblk = be_ref.shape[0]

    def gather_done(xb, sem):
        pltpu.make_async_copy(h_hbm.at[pl.ds(0, tm), :], xb, sem).wait()

    def scatter_done():
        pltpu.make_async_copy(ybuf0, y_hbm.at[pl.ds(0, tm), :], ssem.at[0]).wait()

    @pl.when(i == 0)
    def _():
        ybuf1[...] = jnp.zeros(ybuf1.shape, F32)

        def body(r, carry):
            _row_copy(h_hbm, tokc_ref[0, 0, r], xbuf0, r, gsem.at[0]).start()
            return carry
        lax.fori_loop(0, tm, body, 0)

    @pl.when(i >= 1)
    def _():
        scatter_done()

    def step(x_cur, x_nxt, y_cur, y_prev, g_cur, g_nxt):
        gather_done(x_cur, g_cur)
        for r in range(tm):
            _row_copy(h_hbm, tokn_ref[0, 0, r], x_nxt, r, g_nxt).start()
            _row_copy(y_prev, r, y_hbm, dstp_ref[0, 0, r], ssem.at[0]).start()
        x = x_cur[...].astype(BF16)
        g = jnp.dot(x, wg_ref[0, 0].astype(BF16), preferred_element_type=F32)
        u = jnp.dot(x, wu_ref[0, 0].astype(BF16), preferred_element_type=F32)
        hid = (g * _sigmoid(g) * u).astype(BF16)
        y_cur[...] = jnp.dot(hid, wd_ref[0, 0].astype(BF16), preferred_element_type=F32)

    @pl.when(i % 2 == 0)
    def _():
        step(xbuf0, xbuf1, ybuf0, ybuf1, gsem.at[0], gsem.at[1])

    @pl.when(i % 2 == 1)
    def _():
        step(xbuf1, xbuf0, ybuf1, ybuf0, gsem.at[1], gsem.at[0])

    last = (nblk - 1) % 2
    x_extra, g_extra, y_last = (xbuf1, gsem.at[1], ybuf0) if last == 0 else (xbuf0, gsem.at[0], ybuf1)

    @pl.when(i == nblk - 1)
    def _():
        gather_done(x_extra, g_extra)
        scatter_done()

        def body(r, carry):
            _row_copy(y_last, r, y_hbm, dstc_ref[0, 0, r], ssem.at[0]).start()
            return carry
        lax.fori_loop(0, tm, body, 0)
        scatter_done()


def _experts(h2, slot_tok, slot_dst, blk_expert, w_gate, w_up, w_down, layer, tm, y_rows):
    t, d = h2.shape
    de = w_gate.shape[3]
    nblk = blk_expert.shape[0]
    tok3 = slot_tok.reshape(nblk, 1, tm)
    dst3 = slot_dst.reshape(nblk + 1, 1, tm)
    smem_blk = lambda f: pl.BlockSpec((1, 1, tm), f, memory_space=pltpu.SMEM)
    grid_spec = pltpu.PrefetchScalarGridSpec(
        num_scalar_prefetch=1,
        grid=(nblk,),
        in_specs=[smem_blk(lambda i, be: (i, 0, 0)),
                  smem_blk(lambda i, be: (jnp.minimum(i + 1, nblk - 1), 0, 0)),
                  smem_blk(lambda i, be: (i, 0, 0)),
                  smem_blk(lambda i, be: (i + 1, 0, 0)),
                  pl.BlockSpec(memory_space=pl.ANY),
                  pl.BlockSpec((1, 1, d, de), lambda i, be: (layer, be[i], 0, 0)),
                  pl.BlockSpec((1, 1, d, de), lambda i, be: (layer, be[i], 0, 0)),
                  pl.BlockSpec((1, 1, de, d), lambda i, be: (layer, be[i], 0, 0))],
        out_specs=pl.BlockSpec(memory_space=pl.ANY),
        scratch_shapes=[pltpu.VMEM((tm, d), F32)] * 4
        + [pltpu.SemaphoreType.DMA((2,)), pltpu.SemaphoreType.DMA((1,))],
    )
    return pl.pallas_call(
        functools.partial(_experts_kernel, tm=tm),
        grid_spec=grid_spec,
        out_shape=jax.ShapeDtypeStruct((y_rows, d), F32),
        compiler_params=_cparams("arbitrary"),
    )(blk_expert, tok3, tok3, dst3, dst3, h2, w_gate, w_up, w_down)


def _combine_kernel(*refs, bt, rows, alpha):
    y_refs = refs[:TOPK]
    x1_ref, h_ref, gw_ref, g2_ref, wg_ref, wu_ref, wd_ref, lg_ref, lb_ref, o_ref = refs[TOPK:]
    x = h_ref[...].astype(BF16)
    g = jnp.dot(x, wg_ref[...], preferred_element_type=F32)
    u = jnp.dot(x, wu_ref[...], preferred_element_type=F32)
    hid = (g * _sigmoid(g) * u).astype(BF16)
    moe = jnp.dot(hid, wd_ref[...], preferred_element_type=F32)
    for j in range(TOPK):
        moe = moe + y_refs[j][...] * gw_ref[:, j:j + 1]
    y = alpha * x1_ref[...] + _per_batch(lambda m, g2: (1.0 + g2) * m, moe, (g2_ref,), bt, rows)
    o_ref[...] = _layer_norm(y, lg_ref[...], lb_ref[...])


def _combine(y_rows, gw, x1, h2, gate2, w_gate, w_up, w_down, ln_g, ln_b, rows_per_batch, alpha):
    t, d = x1.shape
    tt, bt, vec_idx = _batch_tiling(t, rows_per_batch, 128)
    nt = t // tt
    row = pl.BlockSpec((1, d), lambda i: (0, 0))
    tile = pl.BlockSpec((tt, d), lambda i: (i, 0))
    full = lambda a: pl.BlockSpec(a.shape, lambda i: (0, 0))
    slab = lambda j: pl.BlockSpec((tt, d), lambda i: (j * nt + i, 0))
    return pl.pallas_call(
        functools.partial(_combine_kernel, bt=bt, rows=rows_per_batch, alpha=alpha),
        grid=(nt,),
        in_specs=[slab(j) for j in range(TOPK)] + [
            tile, tile, pl.BlockSpec((tt, TOPK), lambda i: (i, 0)),
            pl.BlockSpec((bt, 1, d), lambda i: (vec_idx(i), 0, 0)),
            full(w_gate), full(w_up), full(w_down), row, row],
        out_specs=tile,
        out_shape=jax.ShapeDtypeStruct((t, d), F32),
        compiler_params=_cparams("arbitrary"),
    )(*([y_rows] * TOPK), x1, h2, gw, gate2, w_gate, w_up, w_down, ln_g, ln_b)


def _dispatch_plan(eidx_t, rank_t, counts, tm):
    k, t = eidx_t.shape
    n_experts = counts.shape[0]
    nblk = (t * k + tm - 1) // tm + n_experts
    padded = (counts + tm - 1) // tm * tm
    pend = jnp.cumsum(padded)
    pstart = pend - padded
    pos = (pstart[eidx_t] + rank_t).reshape(-1)
    tok = jnp.broadcast_to(jnp.arange(t, dtype=I32)[None, :], (k, t)).reshape(-1)
    slot_tok = jnp.zeros((nblk * tm,), I32).at[pos].set(tok)
    dump = k * t + jnp.arange(tm, dtype=I32)
    slot_dst = jnp.tile(dump, nblk + 1).at[tm + pos].set(jnp.arange(k * t, dtype=I32))
    blk_row = jnp.arange(nblk, dtype=I32) * tm
    blk_expert = jnp.minimum(jnp.searchsorted(pend, blk_row, side='right'), n_experts - 1).astype(I32)
    return slot_tok, slot_dst, blk_expert


def _layer(x, mods, layer, wts, rows_per_batch, alpha, mix_state):
    b, s, d = x.shape
    t = b * s
    sh1, sc1, g1, sh2, sc2, g2 = mods
    x2 = x.reshape(t, d)
    n_heads = wts['b_f'].shape[1]
    da = n_heads * HEAD_DIM
    dc = d - da
    assert da == dc, "column-block indexing of the projection assumes equal attention / conv widths"
    cb = wts['conv_w'].shape[1] - 1

    p, flog = _in_proj(x2, sh1, sc1, wts['w_in_main'][layer], wts['w_in_f'][layer], rows_per_batch)
    p3 = p.reshape(b, s, p.shape[1])
    k_new = p3[:, :, da:2 * da]
    v_new = p3[:, :, 2 * da:3 * da]
    b_f_row = wts['b_f'][layer][None, :]
    conv_args = (wts['conv_w'][layer], wts['conv_b'][layer][None, :], wts['conv_ln_g'][layer][None, :],
                 wts['conv_ln_b'][layer][None, :])

    if mix_state is None:
        logf, _, aug_q, aug_k = _forget_cumsum(flog.reshape(b, s, n_heads), b_f_row,
                                               jnp.zeros((b, 1, n_heads), F32), True)
        attn = _attention(p3, 0, p3, 1, p3, 2, aug_q, aug_k, 0)
        conv, tail = _conv_module(p3, 3, p3, 3, 4, False, *conv_args)
    else:
        ck, cv, clf, cconv = mix_state
        past = ck.shape[1]
        _, cum_c, _, aug_kc = _forget_cumsum(clf, b_f_row, jnp.zeros((b, 1, n_heads), F32), False)
        logf, _, aug_q, aug_kn = _forget_cumsum(flog.reshape(b, s, n_heads), b_f_row,
                                                cum_c[:, past - 1:past, :], True)
        tk = min(512, past)
        sk = -(-(past + s) // tk) * tk
        pad = ((0, 0), (0, sk - past - s), (0, 0))
        k_all = jnp.pad(jnp.concatenate([ck.reshape(b, past, da), k_new], axis=1), pad)
        v_all = jnp.pad(jnp.concatenate([cv.reshape(b, past, da), v_new], axis=1), pad)
        aug_k = jnp.pad(jnp.concatenate([aug_kc, aug_kn], axis=1), pad)
        attn = _attention(p3, 0, k_all, 0, v_all, 0, aug_q, aug_k, past)
        hist = jnp.pad(cconv, ((0, 0), (CONV_HALO - cb, 0), (0, 0)))
        conv, tail = _conv_module(p3, 3, hist, 0, 0, True, *conv_args)

    x1, h, logits = _out_proj(attn.reshape(t, da), conv.reshape(t, dc), x2, g1, sh2, sc2,
                              wts['w_out_a'][layer], wts['w_out_c'][layer],
                              wts['ln1_g'][layer][None, :], wts['ln1_b'][layer][None, :],
                              wts['wr_hi'][layer], wts['wr_lo'][layer], rows_per_batch, alpha)

    eidx_t, gw_t, rank_t, counts = _route(logits, wts['e_bias'][layer][:, None])
    tm_e = _tile(t, 256)
    slot_tok, slot_dst, blk_expert = _dispatch_plan(eidx_t, rank_t, counts[:, 0].astype(I32), tm_e)
    y_rows = _experts(h, slot_tok, slot_dst, blk_expert, wts['w_e_gate'], wts['w_e_up'], wts['w_e_down'],
                      layer, tm_e, TOPK * t + tm_e)
    x_out = _combine(y_rows, gw_t.T, x1, h, g2, wts['w_s_gate'][layer], wts['w_s_up'][layer],
                     wts['w_s_down'][layer], wts['ln2_g'][layer][None, :], wts['ln2_b'][layer][None, :],
                     rows_per_batch, alpha)

    state = (k_new.reshape(b, s, n_heads, HEAD_DIM), v_new.reshape(b, s, n_heads, HEAD_DIM), logf,
             tail[:, CONV_HALO - cb:, :])
    return x_out.reshape(b, s, d), state


def kernel(x_prompt, x_sample, c_prompt, c_sample, cache_k, cache_v, cache_logf, state_conv, w_ada, b_ada, w_in, b_f, conv_w, conv_b, conv_ln_g, conv_ln_b, w_out, ln1_g, ln1_b, w_router, e_bias, w_e_gate, w_e_up, w_e_down, w_s_gate, w_s_up, w_s_down, ln2_g, ln2_b):
    depth, d, _ = w_in.shape
    n_heads = b_f.shape[1]
    da = n_heads * HEAD_DIM
    n_experts = w_router.shape[2]
    alpha = (2 * depth) ** 0.25
    bp = x_prompt.shape[0]

    fo = 3 * da
    wr_pad = jnp.pad(w_router, ((0, 0), (0, 0), (0, ROUTER_LANES - n_experts)))
    wr_hi = wr_pad.astype(BF16)
    wts = dict(
        w_in_main=jnp.concatenate([w_in[:, :, :fo], w_in[:, :, fo + n_heads:]], axis=2).astype(BF16),
        w_in_f=w_in[:, :, fo:fo + n_heads].astype(BF16),
        b_f=b_f, conv_w=conv_w, conv_b=conv_b, conv_ln_g=conv_ln_g, conv_ln_b=conv_ln_b,
        w_out_a=w_out[:, :da, :].astype(BF16), w_out_c=w_out[:, da:, :].astype(BF16),
        ln1_g=ln1_g, ln1_b=ln1_b, ln2_g=ln2_g, ln2_b=ln2_b,
        wr_hi=wr_hi, wr_lo=(wr_pad - wr_hi.astype(F32)).astype(BF16), e_bias=e_bias,
        w_e_gate=w_e_gate, w_e_up=w_e_up, w_e_down=w_e_down,
        w_s_gate=w_s_gate.astype(BF16), w_s_up=w_s_up.astype(BF16), w_s_down=w_s_down.astype(BF16),
    )

    mod_all = _adaln(jnp.concatenate([c_prompt, c_sample], axis=0), w_ada, b_ada)

    def mods(layer, lo, hi):
        m = mod_all[layer, lo:hi]
        return tuple(m[:, None, i * d:(i + 1) * d] for i in range(6))

    xp, xs = x_prompt, x_sample
    outs_p, outs_s = [], []
    for layer in range(depth):
        xp, st = _layer(xp, mods(layer, 0, bp), layer, wts, x_prompt.shape[1], alpha, None)
        outs_p.append(st)
        cache = (cache_k[layer], cache_v[layer], cache_logf[layer], state_conv[layer])
        xs, st = _layer(xs, mods(layer, bp, bp + x_sample.shape[0]), layer, wts, x_sample.shape[1], alpha, cache)
        outs_s.append(st)

    stack = lambda outs, i: jnp.stack([o[i] for o in outs])
    return (xp, xs, stack(outs_p, 0), stack(outs_p, 1), stack(outs_p, 2), stack(outs_p, 3),
            stack(outs_s, 0), stack(outs_s, 1), stack(outs_s, 2), stack(outs_s, 3))
```

```python
import functools

import jax
import jax.numpy as jnp
from jax import lax
from jax.experimental import pallas as pl
from jax.experimental.pallas import tpu as pltpu

F32 = jnp.float32
BF16 = jnp.bfloat16
I32 = jnp.int32

HEAD_DIM = 128
N_GROUPS = 8
TOPK_GROUPS = 4
TOPK = 8
ROUTED_SCALE = 2.5
LN_EPS = 1e-5
NEG_BIG = -1e30

CONV_HALO = 32
ROUTER_LANES = 128
VMEM_LIMIT = 56 * 1024 * 1024


def _cparams(*sem):
    return pltpu.CompilerParams(dimension_semantics=sem, vmem_limit_bytes=VMEM_LIMIT)


def _tile(n, target, mult=8):
    if n <= target:
        return n
    t = (target // mult) * mult
    while t >= mult:
        if n % t == 0:
            return t
        t -= mult
    return n


def _sigmoid(x):
    return 1.0 / (1.0 + jnp.exp(-x))


def _log_sigmoid(x):
    return jnp.minimum(x, 0.0) - jnp.log(1.0 + jnp.exp(-jnp.abs(x)))


def _layer_norm(y, g, b):
    mu = jnp.mean(y, axis=-1, keepdims=True)
    d = y - mu
    var = jnp.mean(d * d, axis=-1, keepdims=True)
    return d * lax.rsqrt(var + LN_EPS) * g + b


def _pack_bf16_pair(lo, hi):
    lo_bits = lax.bitcast_convert_type(lo.astype(BF16).astype(F32), jnp.uint32)
    hi_bits = lax.bitcast_convert_type(hi.astype(BF16).astype(F32), jnp.uint32)
    return (lo_bits >> 16) | (hi_bits & jnp.uint32(0xFFFF0000))


def _unpack_bf16_pair(w):
    lo = lax.bitcast_convert_type(w << 16, F32)
    hi = lax.bitcast_convert_type(w & jnp.uint32(0xFFFF0000), F32)
    return lo, hi


def _unpack_rows_bf16(w):
    lo, hi = _unpack_bf16_pair(w)
    return jnp.concatenate([lo.astype(BF16), hi.astype(BF16)], axis=1)


def _per_batch(fn, x, refs, bt, rows):
    if bt == 1:
        return fn(x, *[r[0] for r in refs])
    tm, d = x.shape
    out = fn(x.reshape(bt, rows, d), *[r[...] for r in refs])
    return out.reshape(tm, d)


def _batch_tiling(n_rows, rows_per_batch, target):
    if rows_per_batch >= target:
        tm = _tile(rows_per_batch, target)
        per = rows_per_batch // tm
        return tm, 1, (lambda i: i // per)
    nb = n_rows // rows_per_batch
    bt = _tile(nb, max(1, target // rows_per_batch), 1)
    return bt * rows_per_batch, bt, (lambda i: i)


def _adaln_kernel(c_ref, w_ref, b_ref, o_ref):
    c = c_ref[...]
    a = (c * _sigmoid(c)).astype(BF16)
    o_ref[0] = jnp.dot(a, w_ref[0].astype(BF16), preferred_element_type=F32) + b_ref[0]


def _adaln(c_all, w_ada, b_ada):
    depth, d, n6 = w_ada.shape
    bc = c_all.shape[0]
    tn = _tile(n6, 1024, 128)
    return pl.pallas_call(
        _adaln_kernel,
        grid=(depth, n6 // tn),
        in_specs=[pl.BlockSpec((bc, d), lambda l, j: (0, 0)),
                  pl.BlockSpec((1, d, tn), lambda l, j: (l, 0, j)),
                  pl.BlockSpec((1, 1, tn), lambda l, j: (l, 0, j))],
        out_specs=pl.BlockSpec((1, bc, tn), lambda l, j: (l, 0, j)),
        out_shape=jax.ShapeDtypeStruct((depth, bc, n6), F32),
        compiler_params=_cparams("arbitrary", "arbitrary"),
    )(c_all, w_ada, b_ada.reshape(depth, 1, n6))


def _in_proj_kernel(x_ref, sh_ref, sc_ref, w_ref, wf_ref, p_ref, f_ref, u_ref, *, bt, rows):
    @pl.when(pl.program_id(1) == 0)
    def _():
        u = _per_batch(lambda x, sh, sc: x * (1.0 + sc) + sh, x_ref[...], (sh_ref, sc_ref), bt, rows)
        ub = u.astype(BF16)
        u_ref[...] = ub
        f_ref[...] = jnp.dot(ub, wf_ref[...], preferred_element_type=F32)

    p_ref[...] = jnp.dot(u_ref[...], w_ref[...], preferred_element_type=F32)


def _in_proj(x2, shift, scale, w_main, w_f, rows_per_batch):
    t, d = x2.shape
    n_main = w_main.shape[1]
    h = w_f.shape[1]
    tm, bt, vec_idx = _batch_tiling(t, rows_per_batch, 1024)
    tn = _tile(n_main, 512, 128)
    vec_spec = pl.BlockSpec((bt, 1, d), lambda i, j: (vec_idx(i), 0, 0))
    return pl.pallas_call(
        functools.partial(_in_proj_kernel, bt=bt, rows=rows_per_batch),
        grid=(t // tm, n_main // tn),
        in_specs=[pl.BlockSpec((tm, d), lambda i, j: (i, 0)), vec_spec, vec_spec,
                  pl.BlockSpec((d, tn), lambda i, j: (0, j)),
                  pl.BlockSpec((d, h), lambda i, j: (0, 0))],
        out_specs=[pl.BlockSpec((tm, tn), lambda i, j: (i, j)),
                   pl.BlockSpec((tm, h), lambda i, j: (i, 0))],
        out_shape=[jax.ShapeDtypeStruct((t, n_main), F32), jax.ShapeDtypeStruct((t, h), F32)],
        scratch_shapes=[pltpu.VMEM((tm, d), BF16)],
        compiler_params=_cparams("arbitrary", "arbitrary"),
    )(x2, shift, scale, w_main, w_f)


LOG2E = 1.4426950408889634
AUG_TERMS = 3


def _bias_placement(n_heads):
    rows = (AUG_TERMS + 1) * n_heads
    eq = [[0.0] * (n_heads * HEAD_DIM) for _ in range(rows)]
    ek = [[0.0] * (n_heads * HEAD_DIM) for _ in range(rows)]
    for h in range(n_heads):
        for n in range(AUG_TERMS):
            eq[n * n_heads + h][h * HEAD_DIM + n] = 1.0
            ek[n * n_heads + h][h * HEAD_DIM + AUG_TERMS + n] = -1.0
            eq[AUG_TERMS * n_heads + h][h * HEAD_DIM + AUG_TERMS + n] = 1.0
            ek[AUG_TERMS * n_heads + h][h * HEAD_DIM + n] = 1.0
    return jnp.array(eq, BF16), jnp.array(ek, BF16)


def _cumsum_kernel(z_ref, bf_ref, c0_ref, eq_ref, ek_ref, lf_ref, cum_ref, aq_ref, ak_ref, *, logsig, chunk):
    s, n_heads = z_ref.shape[1], z_ref.shape[2]
    r = lax.broadcasted_iota(I32, (chunk, chunk), 0)
    c = lax.broadcasted_iota(I32, (chunk, chunk), 1)
    tri = (r >= c).astype(F32)
    carry = c0_ref[0]
    for ci in range(s // chunk):
        sl = slice(ci * chunk, (ci + 1) * chunk)
        z = z_ref[0, sl, :]
        lf = _log_sigmoid(z + bf_ref[...]) if logsig else z
        lf_ref[0, sl, :] = lf
        cum = jnp.dot(tri, lf, precision=lax.Precision.HIGHEST, preferred_element_type=F32) + carry
        cum_ref[0, sl, :] = cum
        carry = cum[chunk - 1:chunk, :]
        rem = cum * LOG2E
        pieces = []
        for _ in range(AUG_TERMS):
            piece = rem.astype(BF16).astype(F32)
            pieces.append(piece)
            rem = rem - piece
        pm = jnp.concatenate(pieces + [jnp.ones((chunk, n_heads), F32)], axis=1).astype(BF16)
        aq_ref[0, sl, :] = jnp.dot(pm, eq_ref[...], preferred_element_type=F32).astype(BF16)
        ak_ref[0, sl, :] = jnp.dot(pm, ek_ref[...], preferred_element_type=F32).astype(BF16)


def _forget_cumsum(z, b_f_row, carry0, logsig):
    b, s, h = z.shape
    chunk = _tile(s, 256)
    blk = pl.BlockSpec((1, s, h), lambda i: (i, 0, 0))
    aug = pl.BlockSpec((1, s, h * HEAD_DIM), lambda i: (i, 0, 0))
    eq, ek = _bias_placement(h)
    place = pl.BlockSpec(eq.shape, lambda i: (0, 0))
    return pl.pallas_call(
        functools.partial(_cumsum_kernel, logsig=logsig, chunk=chunk),
        grid=(b,),
        in_specs=[blk, pl.BlockSpec((1, h), lambda i: (0, 0)), pl.BlockSpec((1, 1, h), lambda i: (i, 0, 0)),
                  place, place],
        out_specs=[blk, blk, aug, aug],
        out_shape=[jax.ShapeDtypeStruct((b, s, h), F32)] * 2
        + [jax.ShapeDtypeStruct((b, s, h * HEAD_DIM), BF16)] * 2,
        compiler_params=_cparams("arbitrary"),
    )(z, b_f_row, carry0, eq, ek)


def _attn_kernel(q_ref, k_ref, v_ref, aq_ref, ak_ref, o_ref, m_ref, acc_ref, *, n_heads, tq, tk, q_off, scale):
    qi = pl.program_id(1)
    ki = pl.program_id(2)

    @pl.when(ki == 0)
    def _():
        m_ref[...] = jnp.full(m_ref.shape, NEG_BIG, F32)
        acc_ref[...] = jnp.zeros(acc_ref.shape, F32)

    q_start = qi * tq + q_off
    k_start = ki * tk

    @pl.when(k_start <= q_start + (tq - 1))
    def _():
        qpos = q_start + lax.broadcasted_iota(I32, (tq, tk), 0)
        kpos = k_start + lax.broadcasted_iota(I32, (tq, tk), 1)
        visible = kpos <= qpos
        ones = jnp.ones((tk, HEAD_DIM), BF16)
        for h in range(n_heads):
            hs = slice(h * HEAD_DIM, (h + 1) * HEAD_DIM)
            qa = jnp.concatenate([(q_ref[0, :, hs] * (scale * LOG2E)).astype(BF16), aq_ref[0, :, hs]], axis=1)
            ka = jnp.concatenate([k_ref[0, :, hs].astype(BF16), ak_ref[0, :, hs]], axis=1)
            va = jnp.concatenate([v_ref[0, :, hs].astype(BF16), ones], axis=1)
            s = lax.dot_general(qa, ka, (((1,), (1,)), ((), ())), preferred_element_type=F32)
            s = jnp.where(visible, s, NEG_BIG)
            m_prev = m_ref[h]
            m_new = jnp.maximum(m_prev, jnp.max(s, axis=-1, keepdims=True))
            p = jnp.exp2(s - m_new).astype(BF16)
            acc_ref[h] = jnp.exp2(m_prev - m_new) * acc_ref[h] + jnp.dot(p, va, preferred_element_type=F32)
            m_ref[h] = m_new

    @pl.when(ki == pl.num_programs(2) - 1)
    def _():
        for h in range(n_heads):
            acc = acc_ref[h]
            o_ref[0, :, h * HEAD_DIM:(h + 1) * HEAD_DIM] = (
                acc[:, :HEAD_DIM] / acc[:, HEAD_DIM:HEAD_DIM + 1]).astype(o_ref.dtype)


def _attention(q_arr, q_col, k_arr, k_col, v_arr, v_col, aug_q, aug_k, q_off):
    b, sq, da = aug_q.shape
    sk = aug_k.shape[1]
    n_heads = da // HEAD_DIM
    tq = _tile(sq, 512)
    tk = _tile(sk, 512)

    def last_k(i):
        return (i * tq + q_off + tq - 1) // tk

    kern = functools.partial(_attn_kernel, n_heads=n_heads, tq=tq, tk=tk, q_off=q_off, scale=HEAD_DIM ** -0.5)
    return pl.pallas_call(
        kern,
        grid=(b, sq // tq, sk // tk),
        in_specs=[pl.BlockSpec((1, tq, da), lambda bi, i, j: (bi, i, q_col)),
                  pl.BlockSpec((1, tk, da), lambda bi, i, j: (bi, jnp.minimum(j, last_k(i)), k_col)),
                  pl.BlockSpec((1, tk, da), lambda bi, i, j: (bi, jnp.minimum(j, last_k(i)), v_col)),
                  pl.BlockSpec((1, tq, da), lambda bi, i, j: (bi, i, 0)),
                  pl.BlockSpec((1, tk, da), lambda bi, i, j: (bi, jnp.minimum(j, last_k(i)), 0))],
        out_specs=pl.BlockSpec((1, tq, da), lambda bi, i, j: (bi, i, 0)),
        out_shape=jax.ShapeDtypeStruct((b, sq, da), BF16),
        scratch_shapes=[pltpu.VMEM((n_heads, tq, 1), F32), pltpu.VMEM((n_heads, tq, 2 * HEAD_DIM), F32)],
        compiler_params=_cparams("arbitrary", "arbitrary", "arbitrary"),
    )(q_arr, k_arr, v_arr, aug_q, aug_k)


SUBLANES = 8


def _conv_kernel(a_ref, b_ref, ha_ref, hb_ref, w_ref, cb_ref, g_ref, be_ref, o_ref, tail_ref, xbuf, xsh, ybuf,
                 *, tt, kw, halo_is_glu, zero_first):
    glu = a_ref[0] * _sigmoid(b_ref[0])
    xbuf[CONV_HALO:CONV_HALO + tt, :] = glu
    halo = ha_ref[0] if halo_is_glu else ha_ref[0] * _sigmoid(hb_ref[0])
    if zero_first:
        halo = jnp.where(pl.program_id(1) == 0, 0.0, halo)
    xbuf[0:CONV_HALO, :] = halo
    tail_ref[0] = xbuf[tt:tt + CONV_HALO, :]

    span = tt + CONV_HALO - SUBLANES
    for s in range(1, SUBLANES):
        xsh[s - 1, 0:span, :] = xbuf[s:s + span, :]

    channels = xbuf.shape[1]
    rc = _tile(tt, 64)
    first = CONV_HALO - (kw - 1)
    for c0 in range(0, channels, 128):
        cs = slice(c0, c0 + 128)
        for r0 in range(0, tt, rc):
            acc = jnp.zeros((rc, 128), F32)
            for j in range(kw):
                shift = (first + j) % SUBLANES
                base = (first + j) - shift + r0
                src = xbuf if shift == 0 else xsh.at[shift - 1]
                acc = acc + src[base:base + rc, cs] * w_ref[j:j + 1, cs]
            ybuf[r0:r0 + rc, cs] = acc
    y = _layer_norm(ybuf[...] + cb_ref[...], g_ref[...], be_ref[...])
    o_ref[0] = (y * _sigmoid(y)).astype(o_ref.dtype)


def _conv_module(p3, a_col, halo_arr, halo_a_col, halo_b_col, halo_is_glu, conv_w, conv_b, ln_g, ln_b):
    b, s, _ = p3.shape
    kw, c = conv_w.shape
    assert kw - 1 <= CONV_HALO
    tt = _tile(s, 128)
    per = tt // CONV_HALO if tt >= CONV_HALO else 1

    def halo_row(i):
        return jnp.maximum(i * per - 1, 0)

    kern = functools.partial(_conv_kernel, tt=tt, kw=kw, halo_is_glu=halo_is_glu, zero_first=not halo_is_glu)
    vec = pl.BlockSpec((1, c), lambda bi, i: (0, 0))
    return pl.pallas_call(
        kern,
        grid=(b, s // tt),
        in_specs=[pl.BlockSpec((1, tt, c), lambda bi, i: (bi, i, a_col)),
                  pl.BlockSpec((1, tt, c), lambda bi, i: (bi, i, a_col + 1)),
                  pl.BlockSpec((1, CONV_HALO, c), lambda bi, i: (bi, halo_row(i), halo_a_col)),
                  pl.BlockSpec((1, CONV_HALO, c), lambda bi, i: (bi, halo_row(i), halo_b_col)),
                  pl.BlockSpec((kw, c), lambda bi, i: (0, 0)), vec, vec, vec],
        out_specs=[pl.BlockSpec((1, tt, c), lambda bi, i: (bi, i, 0)),
                   pl.BlockSpec((1, CONV_HALO, c), lambda bi, i: (bi, 0, 0))],
        out_shape=[jax.ShapeDtypeStruct((b, s, c), BF16), jax.ShapeDtypeStruct((b, CONV_HALO, c), F32)],
        scratch_shapes=[pltpu.VMEM((CONV_HALO + tt, c), F32), pltpu.VMEM((SUBLANES - 1, CONV_HALO + tt, c), F32),
                        pltpu.VMEM((tt, c), F32)],
        compiler_params=_cparams("arbitrary", "arbitrary"),
    )(p3, p3, halo_arr, halo_arr, conv_w, conv_b, ln_g, ln_b)


def _out_proj_kernel(at_ref, cv_ref, x_ref, g1_ref, sh2_ref, sc2_ref, wa_ref, wc_ref, lg_ref, lb_ref,
                     wrh_ref, wrl_ref, x1_ref, h_ref, lo_ref, *, bt, rows, alpha):
    mix = (jnp.dot(at_ref[...], wa_ref[...], preferred_element_type=F32)
           + jnp.dot(cv_ref[...], wc_ref[...], preferred_element_type=F32))
    y = alpha * x_ref[...] + _per_batch(lambda m, g: (1.0 + g) * m, mix, (g1_ref,), bt, rows)
    x1 = _layer_norm(y, lg_ref[...], lb_ref[...])
    x1_ref[...] = x1
    h = _per_batch(lambda x, sh, sc: x * (1.0 + sc) + sh, x1, (sh2_ref, sc2_ref), bt, rows)
    half = h.shape[1] // 2
    h_ref[...] = _pack_bf16_pair(h[:, :half], h[:, half:])
    hh = h.astype(BF16)
    hl = (h - hh.astype(F32)).astype(BF16)
    lo_ref[...] = (jnp.dot(hh, wrh_ref[...], preferred_element_type=F32)
                   + jnp.dot(hh, wrl_ref[...], preferred_element_type=F32)
                   + jnp.dot(hl, wrh_ref[...], preferred_element_type=F32))


def _out_proj(attn2, conv2, x2, gate1, shift2, scale2, w_out_a, w_out_c, ln_g, ln_b, wr_hi, wr_lo,
              rows_per_batch, alpha):
    t, d = x2.shape
    da = attn2.shape[1]
    dc = conv2.shape[1]
    tm, bt, vec_idx = _batch_tiling(t, rows_per_batch, 256)
    vec_spec = pl.BlockSpec((bt, 1, d), lambda i: (vec_idx(i), 0, 0))
    row = pl.BlockSpec((1, d), lambda i: (0, 0))
    full = lambda a: pl.BlockSpec(a.shape, lambda i: (0, 0))
    return pl.pallas_call(
        functools.partial(_out_proj_kernel, bt=bt, rows=rows_per_batch, alpha=alpha),
        grid=(t // tm,),
        in_specs=[pl.BlockSpec((tm, da), lambda i: (i, 0)), pl.BlockSpec((tm, dc), lambda i: (i, 0)),
                  pl.BlockSpec((tm, d), lambda i: (i, 0)), vec_spec, vec_spec, vec_spec,
                  full(w_out_a), full(w_out_c), row, row, full(wr_hi), full(wr_lo)],
        out_specs=[pl.BlockSpec((tm, d), lambda i: (i, 0)), pl.BlockSpec((tm, d // 2), lambda i: (i, 0)),
                   pl.BlockSpec((tm, ROUTER_LANES), lambda i: (i, 0))],
        out_shape=[jax.ShapeDtypeStruct((t, d), F32), jax.ShapeDtypeStruct((t, d // 2), jnp.uint32),
                   jax.ShapeDtypeStruct((t, ROUTER_LANES), F32)],
        compiler_params=_cparams("arbitrary"),
    )(attn2, conv2, x2, gate1, shift2, scale2, w_out_a, w_out_c, ln_g, ln_b, wr_hi, wr_lo)


def _route_kernel(lo_ref, eb_ref, eidx_ref, gw_ref, rank_ref, cnt_ref, carry_ref, *, n_experts, tr):
    i = pl.program_id(0)
    gsz = n_experts // N_GROUPS

    @pl.when(i == 0)
    def _():
        carry_ref[...] = jnp.zeros(carry_ref.shape, F32)

    logits = lo_ref[...].T[0:n_experts, :]
    sub = lax.broadcasted_iota(I32, (gsz, tr), 0)
    s_g, sb_g, gid_g = [], [], []
    for g in range(N_GROUPS):
        rs = slice(g * gsz, (g + 1) * gsz)
        s = _sigmoid(logits[rs, :])
        s_g.append(s)
        sb_g.append(s + eb_ref[rs, :])
        gid_g.append(sub + g * gsz)

    gscore = []
    for g in range(N_GROUPS):
        x = sb_g[g]
        m1 = jnp.max(x, axis=0, keepdims=True)
        i1 = jnp.min(jnp.where(x == m1, sub, gsz), axis=0, keepdims=True)
        m2 = jnp.max(jnp.where(sub == i1, -jnp.inf, x), axis=0, keepdims=True)
        gscore.append(m1 + m2)
    gs = jnp.concatenate(gscore, axis=0)
    gi = lax.broadcasted_iota(I32, (N_GROUPS, tr), 0)
    sel = jnp.zeros((N_GROUPS, tr), F32)
    for _ in range(TOPK_GROUPS):
        gm = jnp.max(gs, axis=0, keepdims=True)
        pick = gi == jnp.min(jnp.where(gs == gm, gi, N_GROUPS), axis=0, keepdims=True)
        sel = jnp.where(pick, 1.0, sel)
        gs = jnp.where(pick, -jnp.inf, gs)

    cand = [jnp.where(sel[g:g + 1, :] > 0.0, sb_g[g], -jnp.inf) for g in range(N_GROUPS)]
    member = [jnp.zeros((gsz, tr), F32) for _ in range(N_GROUPS)]
    eidx, wsel = [], []
    for _ in range(TOPK):
        mx = cand[0]
        for g in range(1, N_GROUPS):
            mx = jnp.maximum(mx, cand[g])
        mx = jnp.max(mx, axis=0, keepdims=True)
        ei = jnp.where(cand[0] == mx, gid_g[0], n_experts)
        for g in range(1, N_GROUPS):
            ei = jnp.minimum(ei, jnp.where(cand[g] == mx, gid_g[g], n_experts))
        ei = jnp.min(ei, axis=0, keepdims=True)
        w = jnp.zeros((gsz, tr), F32)
        for g in range(N_GROUPS):
            hit = gid_g[g] == ei
            w = w + jnp.where(hit, s_g[g], 0.0)
            member[g] = jnp.where(hit, 1.0, member[g])
            cand[g] = jnp.where(hit, -jnp.inf, cand[g])
        eidx.append(ei)
        wsel.append(jnp.sum(w, axis=0, keepdims=True))
    w8 = jnp.concatenate(wsel, axis=0)
    gw_ref[...] = w8 / jnp.sum(w8, axis=0, keepdims=True) * ROUTED_SCALE
    eidx_ref[...] = jnp.concatenate(eidx, axis=0)

    mt = jnp.concatenate(member, axis=0)
    r = lax.broadcasted_iota(I32, (tr, tr), 0)
    c = lax.broadcasted_iota(I32, (tr, tr), 1)
    before = (r < c).astype(BF16)
    prior = jnp.dot(mt.astype(BF16), before, preferred_element_type=F32) + carry_ref[...]
    eio = lax.broadcasted_iota(I32, (n_experts, tr), 0)
    ranks = [jnp.sum(jnp.where(eio == ei, prior, 0.0), axis=0, keepdims=True) for ei in eidx]
    rank_ref[...] = jnp.concatenate(ranks, axis=0).astype(I32)
    carry_ref[...] = carry_ref[...] + jnp.sum(mt, axis=1, keepdims=True)
    cnt_ref[...] = carry_ref[...]


def _route(logits, e_bias_col):
    t = logits.shape[0]
    n_experts = e_bias_col.shape[0]
    tr = _tile(t, 512, 128)
    out_blk = pl.BlockSpec((TOPK, tr), lambda i: (0, i))
    return pl.pallas_call(
        functools.partial(_route_kernel, n_experts=n_experts, tr=tr),
        grid=(t // tr,),
        in_specs=[pl.BlockSpec((tr, ROUTER_LANES), lambda i: (i, 0)),
                  pl.BlockSpec((n_experts, 1), lambda i: (0, 0))],
        out_specs=[out_blk, out_blk, out_blk, pl.BlockSpec((n_experts, 1), lambda i: (0, 0))],
        out_shape=[jax.ShapeDtypeStruct((TOPK, t), I32), jax.ShapeDtypeStruct((TOPK, t), F32),
                   jax.ShapeDtypeStruct((TOPK, t), I32), jax.ShapeDtypeStruct((n_experts, 1), F32)],
        scratch_shapes=[pltpu.VMEM((n_experts, 1), F32)],
        compiler_params=_cparams("arbitrary"),
    )(logits, e_bias_col)


def _row_copy(src, src_row, dst, dst_row, sem):
    return pltpu.make_async_copy(src.at[pl.ds(src_row, 1), :], dst.at[pl.ds(dst_row, 1), :], sem)


def _experts_kernel(be_ref, tokc_ref, tokn_ref, dstp_ref, dstc_ref, h_hbm, wg_ref, wu_ref, wd_ref, y_hbm,
                    xbuf0, xbuf1, ybuf0, ybuf1, gsem, ssem, *, tm):
    i = pl.program_id(0)
    nblk = be_ref.shape[0]

    def gather_done(xb, sem):
        pltpu.make_async_copy(h_hbm.at[pl.ds(0, tm), :], xb, sem).wait()

    def scatter_done():
        pltpu.make_async_copy(ybuf0, y_hbm.at[pl.ds(0, tm), :], ssem.at[0]).wait()

    @pl.when(i == 0)
    def _():
        ybuf1[...] = jnp.zeros(ybuf1.shape, ybuf1.dtype)

        def body(r, carry):
            _row_copy(h_hbm, tokc_ref[0, 0, r], xbuf0, r, gsem.at[0]).start()
            return carry
        lax.fori_loop(0, tm, body, 0)

    @pl.when(i >= 1)
    def _():
        scatter_done()

    def step(x_cur, x_nxt, y_cur, y_prev, g_cur, g_nxt):
        gather_done(x_cur, g_cur)
        for r in range(tm):
            _row_copy(h_hbm, tokn_ref[0, 0, r], x_nxt, r, g_nxt).start()
            _row_copy(y_prev, r, y_hbm, dstp_ref[0, 0, r], ssem.at[0]).start()
        x = _unpack_rows_bf16(x_cur[...])
        g = jnp.dot(x, wg_ref[0, 0].astype(BF16), preferred_element_type=F32)
        u = jnp.dot(x, wu_ref[0, 0].astype(BF16), preferred_element_type=F32)
        hid = (g * _sigmoid(g) * u).astype(BF16)
        y = jnp.dot(hid, wd_ref[0, 0].astype(BF16), preferred_element_type=F32)
        half = y.shape[1] // 2
        y_cur[...] = _pack_bf16_pair(y[:, :half], y[:, half:])

    @pl.when(i % 2 == 0)
    def _():
        step(xbuf0, xbuf1, ybuf0, ybuf1, gsem.at[0], gsem.at[1])

    @pl.when(i % 2 == 1)
    def _():
        step(xbuf1, xbuf0, ybuf1, ybuf0, gsem.at[1], gsem.at[0])

    last = (nblk - 1) % 2
    x_extra, g_extra, y_last = (xbuf1, gsem.at[1], ybuf0) if last == 0 else (xbuf0, gsem.at[0], ybuf1)

    @pl.when(i == nblk - 1)
    def _():
        gather_done(x_extra, g_extra)
        scatter_done()

        def body(r, carry):
            _row_copy(y_last, r, y_hbm, dstc_ref[0, 0, r], ssem.at[0]).start()
            return carry
        lax.fori_loop(0, tm, body, 0)
        scatter_done()


def _experts(h2, slot_tok, slot_dst, blk_expert, w_gate, w_up, w_down, layer, tm, y_rows):
    d, de = w_gate.shape[2], w_gate.shape[3]
    dp = h2.shape[1]
    nblk = blk_expert.shape[0]
    tok3 = slot_tok.reshape(nblk, 1, tm)
    dst3 = slot_dst.reshape(nblk + 1, 1, tm)
    smem_blk = lambda f: pl.BlockSpec((1, 1, tm), f, memory_space=pltpu.SMEM)
    grid_spec = pltpu.PrefetchScalarGridSpec(
        num_scalar_prefetch=1,
        grid=(nblk,),
        in_specs=[smem_blk(lambda i, be: (i, 0, 0)),
                  smem_blk(lambda i, be: (jnp.minimum(i + 1, nblk - 1), 0, 0)),
                  smem_blk(lambda i, be: (i, 0, 0)),
                  smem_blk(lambda i, be: (i + 1, 0, 0)),
                  pl.BlockSpec(memory_space=pl.ANY),
                  pl.BlockSpec((1, 1, d, de), lambda i, be: (layer, be[i], 0, 0)),
                  pl.BlockSpec((1, 1, d, de), lambda i, be: (layer, be[i], 0, 0)),
                  pl.BlockSpec((1, 1, de, d), lambda i, be: (layer, be[i], 0, 0))],
        out_specs=pl.BlockSpec(memory_space=pl.ANY),
        scratch_shapes=[pltpu.VMEM((tm, dp), jnp.uint32)] * 4
        + [pltpu.SemaphoreType.DMA((2,)), pltpu.SemaphoreType.DMA((1,))],
    )
    return pl.pallas_call(
        functools.partial(_experts_kernel, tm=tm),
        grid_spec=grid_spec,
        out_shape=jax.ShapeDtypeStruct((y_rows, dp), jnp.uint32),
        compiler_params=_cparams("arbitrary"),
    )(blk_expert, tok3, tok3, dst3, dst3, h2, w_gate, w_up, w_down)


def _combine_kernel(*refs, bt, rows, alpha):
    y_refs = refs[:TOPK]
    x1_ref, h_ref, gw_ref, g2_ref, wg_ref, wu_ref, wd_ref, lg_ref, lb_ref, o_ref = refs[TOPK:]
    x = _unpack_rows_bf16(h_ref[...])
    g = jnp.dot(x, wg_ref[...], preferred_element_type=F32)
    u = jnp.dot(x, wu_ref[...], preferred_element_type=F32)
    hid = (g * _sigmoid(g) * u).astype(BF16)
    moe = jnp.dot(hid, wd_ref[...], preferred_element_type=F32)
    half = moe.shape[1] // 2
    moe_lo, moe_hi = moe[:, :half], moe[:, half:]
    for j in range(TOPK):
        y_lo, y_hi = _unpack_bf16_pair(y_refs[j][...])
        w = gw_ref[:, j:j + 1]
        moe_lo = moe_lo + y_lo * w
        moe_hi = moe_hi + y_hi * w
    moe = jnp.concatenate([moe_lo, moe_hi], axis=1)
    y = alpha * x1_ref[...] + _per_batch(lambda m, g2: (1.0 + g2) * m, moe, (g2_ref,), bt, rows)
    o_ref[...] = _layer_norm(y, lg_ref[...], lb_ref[...])


def _combine(y_rows, gw, x1, h2, gate2, w_gate, w_up, w_down, ln_g, ln_b, rows_per_batch, alpha):
    t, d = x1.shape
    dp = h2.shape[1]
    tt, bt, vec_idx = _batch_tiling(t, rows_per_batch, 128)
    nt = t // tt
    row = pl.BlockSpec((1, d), lambda i: (0, 0))
    tile = pl.BlockSpec((tt, d), lambda i: (i, 0))
    packed_tile = pl.BlockSpec((tt, dp), lambda i: (i, 0))
    full = lambda a: pl.BlockSpec(a.shape, lambda i: (0, 0))
    slab = lambda j: pl.BlockSpec((tt, dp), lambda i: (j * nt + i, 0))
    return pl.pallas_call(
        functools.partial(_combine_kernel, bt=bt, rows=rows_per_batch, alpha=alpha),
        grid=(nt,),
        in_specs=[slab(j) for j in range(TOPK)] + [
            tile, packed_tile, pl.BlockSpec((tt, TOPK), lambda i: (i, 0)),
            pl.BlockSpec((bt, 1, d), lambda i: (vec_idx(i), 0, 0)),
            full(w_gate), full(w_up), full(w_down), row, row],
        out_specs=tile,
        out_shape=jax.ShapeDtypeStruct((t, d), F32),
        compiler_params=_cparams("arbitrary"),
    )(*([y_rows] * TOPK), x1, h2, gw, gate2, w_gate, w_up, w_down, ln_g, ln_b)


def _dispatch_plan(eidx_t, rank_t, counts, tm):
    k, t = eidx_t.shape
    n_experts = counts.shape[0]
    nblk = (t * k + tm - 1) // tm + n_experts
    padded = (counts + tm - 1) // tm * tm
    pend = jnp.cumsum(padded)
    pstart = pend - padded
    experts = jnp.arange(n_experts, dtype=I32)
    start_of = jnp.sum(jnp.where(eidx_t[:, :, None] == experts, pstart, 0), axis=-1)
    pos = (start_of + rank_t).reshape(-1)
    flat = jnp.full((nblk * tm,), -1, I32).at[pos].set(jnp.arange(k * t, dtype=I32), unique_indices=True)
    real = flat >= 0
    slot_tok = jnp.where(real, flat % t, 0)
    dump = k * t + jnp.arange(nblk * tm, dtype=I32) % tm
    slot_dst = jnp.concatenate([dump[:tm], jnp.where(real, flat, dump)])
    blk_row = jnp.arange(nblk, dtype=I32) * tm
    blk_expert = jnp.minimum(jnp.sum((pend[None, :] <= blk_row[:, None]).astype(I32), axis=1), n_experts - 1)
    return slot_tok, slot_dst, blk_expert


def _layer(x, mods, layer, wts, rows_per_batch, alpha, mix_state):
    b, s, d = x.shape
    t = b * s
    sh1, sc1, g1, sh2, sc2, g2 = mods
    x2 = x.reshape(t, d)
    n_heads = wts['b_f'].shape[1]
    da = n_heads * HEAD_DIM
    dc = d - da
    assert da == dc, "column-block indexing of the projection assumes equal attention / conv widths"
    cb = wts['conv_w'].shape[1] - 1

    p, flog = _in_proj(x2, sh1, sc1, wts['w_in_main'][layer], wts['w_in_f'][layer], rows_per_batch)
    p3 = p.reshape(b, s, p.shape[1])
    k_new = p3[:, :, da:2 * da]
    v_new = p3[:, :, 2 * da:3 * da]
    b_f_row = wts['b_f'][layer][None, :]
    conv_args = (wts['conv_w'][layer], wts['conv_b'][layer][None, :], wts['conv_ln_g'][layer][None, :],
                 wts['conv_ln_b'][layer][None, :])

    if mix_state is None:
        logf, _, aug_q, aug_k = _forget_cumsum(flog.reshape(b, s, n_heads), b_f_row,
                                               jnp.zeros((b, 1, n_heads), F32), True)
        attn = _attention(p3, 0, p3, 1, p3, 2, aug_q, aug_k, 0)
        conv, tail = _conv_module(p3, 3, p3, 3, 4, False, *conv_args)
    else:
        ck, cv, clf, cconv = mix_state
        past = ck.shape[1]
        _, cum_c, _, aug_kc = _forget_cumsum(clf, b_f_row, jnp.zeros((b, 1, n_heads), F32), False)
        logf, _, aug_q, aug_kn = _forget_cumsum(flog.reshape(b, s, n_heads), b_f_row,
                                                cum_c[:, past - 1:past, :], True)
        tk = min(512, past)
        sk = -(-(past + s) // tk) * tk
        pad = ((0, 0), (0, sk - past - s), (0, 0))
        k_all = jnp.pad(jnp.concatenate([ck.reshape(b, past, da), k_new], axis=1), pad)
        v_all = jnp.pad(jnp.concatenate([cv.reshape(b, past, da), v_new], axis=1), pad)
        aug_k = jnp.pad(jnp.concatenate([aug_kc, aug_kn], axis=1), pad)
        attn = _attention(p3, 0, k_all, 0, v_all, 0, aug_q, aug_k, past)
        hist = jnp.pad(cconv, ((0, 0), (CONV_HALO - cb, 0), (0, 0)))
        conv, tail = _conv_module(p3, 3, hist, 0, 0, True, *conv_args)

    x1, h, logits = _out_proj(attn.reshape(t, da), conv.reshape(t, dc), x2, g1, sh2, sc2,
                              wts['w_out_a'][layer], wts['w_out_c'][layer],
                              wts['ln1_g'][layer][None, :], wts['ln1_b'][layer][None, :],
                              wts['wr_hi'][layer], wts['wr_lo'][layer], rows_per_batch, alpha)

    eidx_t, gw_t, rank_t, counts = _route(logits, wts['e_bias'][layer][:, None])
    tm_e = _tile(t, 256)
    slot_tok, slot_dst, blk_expert = _dispatch_plan(eidx_t, rank_t, counts[:, 0].astype(I32), tm_e)
    y_rows = _experts(h, slot_tok, slot_dst, blk_expert, wts['w_e_gate'], wts['w_e_up'], wts['w_e_down'],
                      layer, tm_e, TOPK * t + tm_e)
    x_out = _combine(y_rows, gw_t.T, x1, h, g2, wts['w_s_gate'][layer], wts['w_s_up'][layer],
                     wts['w_s_down'][layer], wts['ln2_g'][layer][None, :], wts['ln2_b'][layer][None, :],
                     rows_per_batch, alpha)

    state = (k_new.reshape(b, s, n_heads, HEAD_DIM), v_new.reshape(b, s, n_heads, HEAD_DIM), logf,
             tail[:, CONV_HALO - cb:, :])
    return x_out.reshape(b, s, d), state


def kernel(x_prompt, x_sample, c_prompt, c_sample, cache_k, cache_v, cache_logf, state_conv, w_ada, b_ada, w_in, b_f, conv_w, conv_b, conv_ln_g, conv_ln_b, w_out, ln1_g, ln1_b, w_router, e_bias, w_e_gate, w_e_up, w_e_down, w_s_gate, w_s_up, w_s_down, ln2_g, ln2_b):
    depth, d, _ = w_in.shape
    n_heads = b_f.shape[1]
    da = n_heads * HEAD_DIM
    n_experts = w_router.shape[2]
    alpha = (2 * depth) ** 0.25
    bp = x_prompt.shape[0]

    fo = 3 * da
    wr_pad = jnp.pad(w_router, ((0, 0), (0, 0), (0, ROUTER_LANES - n_experts)))
    wr_hi = wr_pad.astype(BF16)
    wts = dict(
        w_in_main=jnp.concatenate([w_in[:, :, :fo], w_in[:, :, fo + n_heads:]], axis=2).astype(BF16),
        w_in_f=w_in[:, :, fo:fo + n_heads].astype(BF16),
        b_f=b_f, conv_w=conv_w, conv_b=conv_b, conv_ln_g=conv_ln_g, conv_ln_b=conv_ln_b,
        w_out_a=w_out[:, :da, :].astype(BF16), w_out_c=w_out[:, da:, :].astype(BF16),
        ln1_g=ln1_g, ln1_b=ln1_b, ln2_g=ln2_g, ln2_b=ln2_b,
        wr_hi=wr_hi, wr_lo=(wr_pad - wr_hi.astype(F32)).astype(BF16), e_bias=e_bias,
        w_e_gate=w_e_gate, w_e_up=w_e_up, w_e_down=w_e_down,
        w_s_gate=w_s_gate.astype(BF16), w_s_up=w_s_up.astype(BF16), w_s_down=w_s_down.astype(BF16),
    )

    mod_all = _adaln(jnp.concatenate([c_prompt, c_sample], axis=0), w_ada, b_ada)

    def mods(layer, lo, hi):
        m = mod_all[layer, lo:hi]
        return tuple(m[:, None, i * d:(i + 1) * d] for i in range(6))

    xp, xs = x_prompt, x_sample
    outs_p, outs_s = [], []
    for layer in range(depth):
        xp, st = _layer(xp, mods(layer, 0, bp), layer, wts, x_prompt.shape[1], alpha, None)
        outs_p.append(st)
        cache = (cache_k[layer], cache_v[layer], cache_logf[layer], state_conv[layer])
        xs, st = _layer(xs, mods(layer, bp, bp + x_sample.shape[0]), layer, wts, x_sample.shape[1], alpha, cache)
        outs_s.append(st)

    stack = lambda outs, i: jnp.stack([o[i] for o in outs])
    return (xp, xs, stack(outs_p, 0), stack(outs_p, 1), stack(outs_p, 2), stack(outs_p, 3),
            stack(outs_s, 0), stack(outs_s, 1), stack(outs_s, 2), stack(outs_s, 3))
```

```python
import functools

import jax
import jax.numpy as jnp
from jax import lax
from jax.experimental import pallas as pl
from jax.experimental.pallas import tpu as pltpu

F32 = jnp.float32
BF16 = jnp.bfloat16
I32 = jnp.int32

HEAD_DIM = 128
N_GROUPS = 8
TOPK_GROUPS = 4
TOPK = 8
ROUTED_SCALE = 2.5
LN_EPS = 1e-5
NEG_BIG = -1e30

CONV_HALO = 32
ROUTER_LANES = 128
VMEM_LIMIT = 56 * 1024 * 1024


def _cparams(*sem):
    return pltpu.CompilerParams(dimension_semantics=sem, vmem_limit_bytes=VMEM_LIMIT)


def _tile(n, target, mult=8):
    if n <= target:
        return n
    t = (target // mult) * mult
    while t >= mult:
        if n % t == 0:
            return t
        t -= mult
    return n


def _sigmoid(x):
    return 1.0 / (1.0 + jnp.exp(-x))


def _log_sigmoid(x):
    return jnp.minimum(x, 0.0) - jnp.log(1.0 + jnp.exp(-jnp.abs(x)))


def _layer_norm(y, g, b):
    mu = jnp.mean(y, axis=-1, keepdims=True)
    d = y - mu
    var = jnp.mean(d * d, axis=-1, keepdims=True)
    return d * lax.rsqrt(var + LN_EPS) * g + b


def _pack_bf16_pair(lo, hi):
    lo_bits = lax.bitcast_convert_type(lo.astype(BF16).astype(F32), jnp.uint32)
    hi_bits = lax.bitcast_convert_type(hi.astype(BF16).astype(F32), jnp.uint32)
    return (lo_bits >> 16) | (hi_bits & jnp.uint32(0xFFFF0000))


def _unpack_bf16_pair(w):
    lo = lax.bitcast_convert_type(w << 16, F32)
    hi = lax.bitcast_convert_type(w & jnp.uint32(0xFFFF0000), F32)
    return lo, hi


def _unpack_rows_bf16(w):
    lo, hi = _unpack_bf16_pair(w)
    return jnp.concatenate([lo.astype(BF16), hi.astype(BF16)], axis=1)


def _per_batch(fn, x, refs, bt, rows):
    if bt == 1:
        return fn(x, *[r[0] for r in refs])
    tm, d = x.shape
    out = fn(x.reshape(bt, rows, d), *[r[...] for r in refs])
    return out.reshape(tm, d)


def _batch_tiling(n_rows, rows_per_batch, target):
    if rows_per_batch >= target:
        tm = _tile(rows_per_batch, target)
        per = rows_per_batch // tm
        return tm, 1, (lambda i: i // per)
    nb = n_rows // rows_per_batch
    bt = _tile(nb, max(1, target // rows_per_batch), 1)
    return bt * rows_per_batch, bt, (lambda i: i)


def _adaln_kernel(c_ref, w_ref, b_ref, o_ref):
    c = c_ref[...]
    a = (c * _sigmoid(c)).astype(BF16)
    o_ref[0] = jnp.dot(a, w_ref[0].astype(BF16), preferred_element_type=F32) + b_ref[0]


def _adaln(c_all, w_ada, b_ada):
    depth, d, n6 = w_ada.shape
    bc = c_all.shape[0]
    tn = _tile(n6, 1024, 128)
    return pl.pallas_call(
        _adaln_kernel,
        grid=(depth, n6 // tn),
        in_specs=[pl.BlockSpec((bc, d), lambda l, j: (0, 0)),
                  pl.BlockSpec((1, d, tn), lambda l, j: (l, 0, j)),
                  pl.BlockSpec((1, 1, tn), lambda l, j: (l, 0, j))],
        out_specs=pl.BlockSpec((1, bc, tn), lambda l, j: (l, 0, j)),
        out_shape=jax.ShapeDtypeStruct((depth, bc, n6), F32),
        compiler_params=_cparams("arbitrary", "arbitrary"),
    )(c_all, w_ada, b_ada.reshape(depth, 1, n6))


def _in_proj_kernel(x_ref, sh_ref, sc_ref, w_ref, wf_ref, p_ref, f_ref, u_ref, *, bt, rows):
    @pl.when(pl.program_id(1) == 0)
    def _():
        u = _per_batch(lambda x, sh, sc: x * (1.0 + sc) + sh, x_ref[...], (sh_ref, sc_ref), bt, rows)
        ub = u.astype(BF16)
        u_ref[...] = ub
        f_ref[...] = jnp.dot(ub, wf_ref[...], preferred_element_type=F32)

    p_ref[...] = jnp.dot(u_ref[...], w_ref[...], preferred_element_type=F32)


def _in_proj(x2, shift, scale, w_main, w_f, rows_per_batch):
    t, d = x2.shape
    n_main = w_main.shape[1]
    h = w_f.shape[1]
    tm, bt, vec_idx = _batch_tiling(t, rows_per_batch, 1024)
    tn = _tile(n_main, 512, 128)
    vec_spec = pl.BlockSpec((bt, 1, d), lambda i, j: (vec_idx(i), 0, 0))
    return pl.pallas_call(
        functools.partial(_in_proj_kernel, bt=bt, rows=rows_per_batch),
        grid=(t // tm, n_main // tn),
        in_specs=[pl.BlockSpec((tm, d), lambda i, j: (i, 0)), vec_spec, vec_spec,
                  pl.BlockSpec((d, tn), lambda i, j: (0, j)),
                  pl.BlockSpec((d, h), lambda i, j: (0, 0))],
        out_specs=[pl.BlockSpec((tm, tn), lambda i, j: (i, j)),
                   pl.BlockSpec((tm, h), lambda i, j: (i, 0))],
        out_shape=[jax.ShapeDtypeStruct((t, n_main), F32), jax.ShapeDtypeStruct((t, h), F32)],
        scratch_shapes=[pltpu.VMEM((tm, d), BF16)],
        compiler_params=_cparams("arbitrary", "arbitrary"),
    )(x2, shift, scale, w_main, w_f)


LOG2E = 1.4426950408889634
AUG_TERMS = 3


def _bias_placement(n_heads):
    rows = (AUG_TERMS + 1) * n_heads
    eq = [[0.0] * (n_heads * HEAD_DIM) for _ in range(rows)]
    ek = [[0.0] * (n_heads * HEAD_DIM) for _ in range(rows)]
    for h in range(n_heads):
        for n in range(AUG_TERMS):
            eq[n * n_heads + h][h * HEAD_DIM + n] = 1.0
            ek[n * n_heads + h][h * HEAD_DIM + AUG_TERMS + n] = -1.0
            eq[AUG_TERMS * n_heads + h][h * HEAD_DIM + AUG_TERMS + n] = 1.0
            ek[AUG_TERMS * n_heads + h][h * HEAD_DIM + n] = 1.0
    return jnp.array(eq, BF16), jnp.array(ek, BF16)


def _cumsum_kernel(z_ref, bf_ref, c0_ref, eq_ref, ek_ref, lf_ref, cum_ref, aq_ref, ak_ref, *, logsig, chunk):
    s, n_heads = z_ref.shape[1], z_ref.shape[2]
    r = lax.broadcasted_iota(I32, (chunk, chunk), 0)
    c = lax.broadcasted_iota(I32, (chunk, chunk), 1)
    tri = (r >= c).astype(F32)
    carry = c0_ref[0]
    for ci in range(s // chunk):
        sl = slice(ci * chunk, (ci + 1) * chunk)
        z = z_ref[0, sl, :]
        lf = _log_sigmoid(z + bf_ref[...]) if logsig else z
        lf_ref[0, sl, :] = lf
        cum = jnp.dot(tri, lf, precision=lax.Precision.HIGHEST, preferred_element_type=F32) + carry
        cum_ref[0, sl, :] = cum
        carry = cum[chunk - 1:chunk, :]
        rem = cum * LOG2E
        pieces = []
        for _ in range(AUG_TERMS):
            piece = rem.astype(BF16).astype(F32)
            pieces.append(piece)
            rem = rem - piece
        pm = jnp.concatenate(pieces + [jnp.ones((chunk, n_heads), F32)], axis=1).astype(BF16)
        aq_ref[0, sl, :] = jnp.dot(pm, eq_ref[...], preferred_element_type=F32).astype(BF16)
        ak_ref[0, sl, :] = jnp.dot(pm, ek_ref[...], preferred_element_type=F32).astype(BF16)


def _forget_cumsum(z, b_f_row, carry0, logsig):
    b, s, h = z.shape
    chunk = _tile(s, 256)
    blk = pl.BlockSpec((1, s, h), lambda i: (i, 0, 0))
    aug = pl.BlockSpec((1, s, h * HEAD_DIM), lambda i: (i, 0, 0))
    eq, ek = _bias_placement(h)
    place = pl.BlockSpec(eq.shape, lambda i: (0, 0))
    return pl.pallas_call(
        functools.partial(_cumsum_kernel, logsig=logsig, chunk=chunk),
        grid=(b,),
        in_specs=[blk, pl.BlockSpec((1, h), lambda i: (0, 0)), pl.BlockSpec((1, 1, h), lambda i: (i, 0, 0)),
                  place, place],
        out_specs=[blk, blk, aug, aug],
        out_shape=[jax.ShapeDtypeStruct((b, s, h), F32)] * 2
        + [jax.ShapeDtypeStruct((b, s, h * HEAD_DIM), BF16)] * 2,
        compiler_params=_cparams("arbitrary"),
    )(z, b_f_row, carry0, eq, ek)


def _attn_kernel(*refs, n_heads, tq, tk, q_off, scale, tail_rows):
    if tail_rows:
        q_ref, k_ref, v_ref, aq_ref, ak_ref, kt_ref, vt_ref, akt_ref, o_ref, m_ref, acc_ref = refs
    else:
        q_ref, k_ref, v_ref, aq_ref, ak_ref, o_ref, m_ref, acc_ref = refs
    qi = pl.program_id(1)
    ki = pl.program_id(2)
    n_main = pl.num_programs(2) - (1 if tail_rows else 0)

    @pl.when(ki == 0)
    def _():
        m_ref[...] = jnp.full(m_ref.shape, NEG_BIG, F32)
        acc_ref[...] = jnp.zeros(acc_ref.shape, F32)

    q_start = qi * tq + q_off

    def attend(kr, vr, akr, k_start, width):
        qpos = q_start + lax.broadcasted_iota(I32, (tq, width), 0)
        kpos = k_start + lax.broadcasted_iota(I32, (tq, width), 1)
        visible = kpos <= qpos
        ones = jnp.ones((width, HEAD_DIM), BF16)
        for h in range(n_heads):
            hs = slice(h * HEAD_DIM, (h + 1) * HEAD_DIM)
            qa = jnp.concatenate([(q_ref[0, :, hs] * (scale * LOG2E)).astype(BF16), aq_ref[0, :, hs]], axis=1)
            ka = jnp.concatenate([kr[0, :, hs].astype(BF16), akr[0, :, hs]], axis=1)
            va = jnp.concatenate([vr[0, :, hs].astype(BF16), ones], axis=1)
            s = lax.dot_general(qa, ka, (((1,), (1,)), ((), ())), preferred_element_type=F32)
            s = jnp.where(visible, s, NEG_BIG)
            m_prev = m_ref[h]
            m_new = jnp.maximum(m_prev, jnp.max(s, axis=-1, keepdims=True))
            p = jnp.exp2(s - m_new).astype(BF16)
            acc_ref[h] = jnp.exp2(m_prev - m_new) * acc_ref[h] + jnp.dot(p, va, preferred_element_type=F32)
            m_ref[h] = m_new

    @pl.when((ki < n_main) & (ki * tk <= q_start + (tq - 1)))
    def _():
        attend(k_ref, v_ref, ak_ref, ki * tk, tk)

    if tail_rows:
        @pl.when(ki == n_main)
        def _():
            attend(kt_ref, vt_ref, akt_ref, n_main * tk, tail_rows)

    @pl.when(ki == pl.num_programs(2) - 1)
    def _():
        for h in range(n_heads):
            acc = acc_ref[h]
            o_ref[0, :, h * HEAD_DIM:(h + 1) * HEAD_DIM] = (
                acc[:, :HEAD_DIM] / acc[:, HEAD_DIM:HEAD_DIM + 1]).astype(o_ref.dtype)


def _attention(q_arr, q_col, k_arr, k_col, v_arr, v_col, aug_q, aug_k, q_off, kv_batch_off=0, tail=None):
    b, sq, da = aug_q.shape
    sk = aug_k.shape[1]
    n_heads = da // HEAD_DIM
    tq = _tile(sq, 512)
    tk = _tile(sk, 512)
    n_main = sk // tk

    def k_blk(i, j):
        return jnp.minimum(jnp.minimum(j, (i * tq + q_off + tq - 1) // tk), n_main - 1)

    in_specs = [pl.BlockSpec((1, tq, da), lambda bi, i, j: (bi, i, q_col)),
                pl.BlockSpec((1, tk, da), lambda bi, i, j: (kv_batch_off + bi, k_blk(i, j), k_col)),
                pl.BlockSpec((1, tk, da), lambda bi, i, j: (kv_batch_off + bi, k_blk(i, j), v_col)),
                pl.BlockSpec((1, tq, da), lambda bi, i, j: (bi, i, 0)),
                pl.BlockSpec((1, tk, da), lambda bi, i, j: (bi, k_blk(i, j), 0))]
    args = [q_arr, k_arr, v_arr, aug_q, aug_k]
    tail_rows = 0
    if tail is not None:
        tk_arr, tk_col, tv_arr, tv_col, t_aug = tail
        tail_rows = t_aug.shape[1]
        in_specs += [pl.BlockSpec((1, tail_rows, da), lambda bi, i, j: (bi, 0, tk_col)),
                     pl.BlockSpec((1, tail_rows, da), lambda bi, i, j: (bi, 0, tv_col)),
                     pl.BlockSpec((1, tail_rows, da), lambda bi, i, j: (bi, 0, 0))]
        args += [tk_arr, tv_arr, t_aug]

    kern = functools.partial(_attn_kernel, n_heads=n_heads, tq=tq, tk=tk, q_off=q_off, scale=HEAD_DIM ** -0.5,
                             tail_rows=tail_rows)
    return pl.pallas_call(
        kern,
        grid=(b, sq // tq, n_main + (1 if tail_rows else 0)),
        in_specs=in_specs,
        out_specs=pl.BlockSpec((1, tq, da), lambda bi, i, j: (bi, i, 0)),
        out_shape=jax.ShapeDtypeStruct((b, sq, da), BF16),
        scratch_shapes=[pltpu.VMEM((n_heads, tq, 1), F32), pltpu.VMEM((n_heads, tq, 2 * HEAD_DIM), F32)],
        compiler_params=_cparams("arbitrary", "arbitrary", "arbitrary"),
    )(*args)


SUBLANES = 8


def _conv_kernel(a_ref, b_ref, ha_ref, hb_ref, w_ref, cb_ref, g_ref, be_ref, o_ref, tail_ref, xbuf, xsh, ybuf,
                 *, tt, kw, halo_is_glu, zero_first):
    glu = a_ref[0] * _sigmoid(b_ref[0])
    xbuf[CONV_HALO:CONV_HALO + tt, :] = glu
    halo = ha_ref[0] if halo_is_glu else ha_ref[0] * _sigmoid(hb_ref[0])
    if zero_first:
        halo = jnp.where(pl.program_id(1) == 0, 0.0, halo)
    xbuf[0:CONV_HALO, :] = halo
    tail_ref[0] = xbuf[tt:tt + CONV_HALO, :]

    span = tt + CONV_HALO - SUBLANES
    for s in range(1, SUBLANES):
        xsh[s - 1, 0:span, :] = xbuf[s:s + span, :]

    channels = xbuf.shape[1]
    rc = _tile(tt, 64)
    first = CONV_HALO - (kw - 1)
    for c0 in range(0, channels, 128):
        cs = slice(c0, c0 + 128)
        for r0 in range(0, tt, rc):
            acc = jnp.zeros((rc, 128), F32)
            for j in range(kw):
                shift = (first + j) % SUBLANES
                base = (first + j) - shift + r0
                src = xbuf if shift == 0 else xsh.at[shift - 1]
                acc = acc + src[base:base + rc, cs] * w_ref[j:j + 1, cs]
            ybuf[r0:r0 + rc, cs] = acc
    y = _layer_norm(ybuf[...] + cb_ref[...], g_ref[...], be_ref[...])
    o_ref[0] = (y * _sigmoid(y)).astype(o_ref.dtype)


def _conv_module(p3, a_col, halo_arr, halo_a_col, halo_b_col, halo_is_glu, conv_w, conv_b, ln_g, ln_b):
    b, s, _ = p3.shape
    kw, c = conv_w.shape
    assert kw - 1 <= CONV_HALO
    tt = _tile(s, 128)
    per = tt // CONV_HALO if tt >= CONV_HALO else 1

    def halo_row(i):
        return jnp.maximum(i * per - 1, 0)

    kern = functools.partial(_conv_kernel, tt=tt, kw=kw, halo_is_glu=halo_is_glu, zero_first=not halo_is_glu)
    vec = pl.BlockSpec((1, c), lambda bi, i: (0, 0))
    return pl.pallas_call(
        kern,
        grid=(b, s // tt),
        in_specs=[pl.BlockSpec((1, tt, c), lambda bi, i: (bi, i, a_col)),
                  pl.BlockSpec((1, tt, c), lambda bi, i: (bi, i, a_col + 1)),
                  pl.BlockSpec((1, CONV_HALO, c), lambda bi, i: (bi, halo_row(i), halo_a_col)),
                  pl.BlockSpec((1, CONV_HALO, c), lambda bi, i: (bi, halo_row(i), halo_b_col)),
                  pl.BlockSpec((kw, c), lambda bi, i: (0, 0)), vec, vec, vec],
        out_specs=[pl.BlockSpec((1, tt, c), lambda bi, i: (bi, i, 0)),
                   pl.BlockSpec((1, CONV_HALO, c), lambda bi, i: (bi, 0, 0))],
        out_shape=[jax.ShapeDtypeStruct((b, s, c), BF16), jax.ShapeDtypeStruct((b, CONV_HALO, c), F32)],
        scratch_shapes=[pltpu.VMEM((CONV_HALO + tt, c), F32), pltpu.VMEM((SUBLANES - 1, CONV_HALO + tt, c), F32),
                        pltpu.VMEM((tt, c), F32)],
        compiler_params=_cparams("arbitrary", "arbitrary"),
    )(p3, p3, halo_arr, halo_arr, conv_w, conv_b, ln_g, ln_b)


def _out_proj_kernel(at_ref, cv_ref, x_ref, g1_ref, sh2_ref, sc2_ref, wa_ref, wc_ref, lg_ref, lb_ref,
                     wrh_ref, wrl_ref, x1_ref, h_ref, lo_ref, *, bt, rows, alpha):
    mix = (jnp.dot(at_ref[...], wa_ref[...], preferred_element_type=F32)
           + jnp.dot(cv_ref[...], wc_ref[...], preferred_element_type=F32))
    y = alpha * x_ref[...] + _per_batch(lambda m, g: (1.0 + g) * m, mix, (g1_ref,), bt, rows)
    x1 = _layer_norm(y, lg_ref[...], lb_ref[...])
    x1_ref[...] = x1
    h = _per_batch(lambda x, sh, sc: x * (1.0 + sc) + sh, x1, (sh2_ref, sc2_ref), bt, rows)
    half = h.shape[1] // 2
    h_ref[...] = _pack_bf16_pair(h[:, :half], h[:, half:])
    hh = h.astype(BF16)
    hl = (h - hh.astype(F32)).astype(BF16)
    lo_ref[...] = (jnp.dot(hh, wrh_ref[...], preferred_element_type=F32)
                   + jnp.dot(hh, wrl_ref[...], preferred_element_type=F32)
                   + jnp.dot(hl, wrh_ref[...], preferred_element_type=F32))


def _out_proj(attn2, conv2, x2, gate1, shift2, scale2, w_out_a, w_out_c, ln_g, ln_b, wr_hi, wr_lo,
              rows_per_batch, alpha):
    t, d = x2.shape
    da = attn2.shape[1]
    dc = conv2.shape[1]
    tm, bt, vec_idx = _batch_tiling(t, rows_per_batch, 256)
    vec_spec = pl.BlockSpec((bt, 1, d), lambda i: (vec_idx(i), 0, 0))
    row = pl.BlockSpec((1, d), lambda i: (0, 0))
    full = lambda a: pl.BlockSpec(a.shape, lambda i: (0, 0))
    return pl.pallas_call(
        functools.partial(_out_proj_kernel, bt=bt, rows=rows_per_batch, alpha=alpha),
        grid=(t // tm,),
        in_specs=[pl.BlockSpec((tm, da), lambda i: (i, 0)), pl.BlockSpec((tm, dc), lambda i: (i, 0)),
                  pl.BlockSpec((tm, d), lambda i: (i, 0)), vec_spec, vec_spec, vec_spec,
                  full(w_out_a), full(w_out_c), row, row, full(wr_hi), full(wr_lo)],
        out_specs=[pl.BlockSpec((tm, d), lambda i: (i, 0)), pl.BlockSpec((tm, d // 2), lambda i: (i, 0)),
                   pl.BlockSpec((tm, ROUTER_LANES), lambda i: (i, 0))],
        out_shape=[jax.ShapeDtypeStruct((t, d), F32), jax.ShapeDtypeStruct((t, d // 2), jnp.uint32),
                   jax.ShapeDtypeStruct((t, ROUTER_LANES), F32)],
        compiler_params=_cparams("arbitrary"),
    )(attn2, conv2, x2, gate1, shift2, scale2, w_out_a, w_out_c, ln_g, ln_b, wr_hi, wr_lo)


def _route_kernel(lo_ref, eb_ref, eidx_ref, gw_ref, rank_ref, cnt_ref, carry_ref, *, n_experts, tr):
    i = pl.program_id(0)
    gsz = n_experts // N_GROUPS

    @pl.when(i == 0)
    def _():
        carry_ref[...] = jnp.zeros(carry_ref.shape, F32)

    logits = lo_ref[...].T[0:n_experts, :]
    sub = lax.broadcasted_iota(I32, (gsz, tr), 0)
    s_g, sb_g, gid_g = [], [], []
    for g in range(N_GROUPS):
        rs = slice(g * gsz, (g + 1) * gsz)
        s = _sigmoid(logits[rs, :])
        s_g.append(s)
        sb_g.append(s + eb_ref[rs, :])
        gid_g.append(sub + g * gsz)

    gscore = []
    for g in range(N_GROUPS):
        x = sb_g[g]
        m1 = jnp.max(x, axis=0, keepdims=True)
        i1 = jnp.min(jnp.where(x == m1, sub, gsz), axis=0, keepdims=True)
        m2 = jnp.max(jnp.where(sub == i1, -jnp.inf, x), axis=0, keepdims=True)
        gscore.append(m1 + m2)
    gs = jnp.concatenate(gscore, axis=0)
    gi = lax.broadcasted_iota(I32, (N_GROUPS, tr), 0)
    sel = jnp.zeros((N_GROUPS, tr), F32)
    for _ in range(TOPK_GROUPS):
        gm = jnp.max(gs, axis=0, keepdims=True)
        pick = gi == jnp.min(jnp.where(gs == gm, gi, N_GROUPS), axis=0, keepdims=True)
        sel = jnp.where(pick, 1.0, sel)
        gs = jnp.where(pick, -jnp.inf, gs)

    cand = [jnp.where(sel[g:g + 1, :] > 0.0, sb_g[g], -jnp.inf) for g in range(N_GROUPS)]
    member = [jnp.zeros((gsz, tr), F32) for _ in range(N_GROUPS)]
    eidx, wsel = [], []
    for _ in range(TOPK):
        mx = cand[0]
        for g in range(1, N_GROUPS):
            mx = jnp.maximum(mx, cand[g])
        mx = jnp.max(mx, axis=0, keepdims=True)
        ei = jnp.where(cand[0] == mx, gid_g[0], n_experts)
        for g in range(1, N_GROUPS):
            ei = jnp.minimum(ei, jnp.where(cand[g] == mx, gid_g[g], n_experts))
        ei = jnp.min(ei, axis=0, keepdims=True)
        w = jnp.zeros((gsz, tr), F32)
        for g in range(N_GROUPS):
            hit = gid_g[g] == ei
            w = w + jnp.where(hit, s_g[g], 0.0)
            member[g] = jnp.where(hit, 1.0, member[g])
            cand[g] = jnp.where(hit, -jnp.inf, cand[g])
        eidx.append(ei)
        wsel.append(jnp.sum(w, axis=0, keepdims=True))
    w8 = jnp.concatenate(wsel, axis=0)
    gw_ref[...] = w8 / jnp.sum(w8, axis=0, keepdims=True) * ROUTED_SCALE
    eidx_ref[...] = jnp.concatenate(eidx, axis=0)

    mt = jnp.concatenate(member, axis=0)
    r = lax.broadcasted_iota(I32, (tr, tr), 0)
    c = lax.broadcasted_iota(I32, (tr, tr), 1)
    before = (r < c).astype(BF16)
    prior = jnp.dot(mt.astype(BF16), before, preferred_element_type=F32) + carry_ref[...]
    eio = lax.broadcasted_iota(I32, (n_experts, tr), 0)
    ranks = [jnp.sum(jnp.where(eio == ei, prior, 0.0), axis=0, keepdims=True) for ei in eidx]
    rank_ref[...] = jnp.concatenate(ranks, axis=0).astype(I32)
    carry_ref[...] = carry_ref[...] + jnp.sum(mt, axis=1, keepdims=True)
    cnt_ref[...] = carry_ref[...]


def _route(logits, e_bias_col):
    t = logits.shape[0]
    n_experts = e_bias_col.shape[0]
    tr = _tile(t, 512, 128)
    out_blk = pl.BlockSpec((TOPK, tr), lambda i: (0, i))
    return pl.pallas_call(
        functools.partial(_route_kernel, n_experts=n_experts, tr=tr),
        grid=(t // tr,),
        in_specs=[pl.BlockSpec((tr, ROUTER_LANES), lambda i: (i, 0)),
                  pl.BlockSpec((n_experts, 1), lambda i: (0, 0))],
        out_specs=[out_blk, out_blk, out_blk, pl.BlockSpec((n_experts, 1), lambda i: (0, 0))],
        out_shape=[jax.ShapeDtypeStruct((TOPK, t), I32), jax.ShapeDtypeStruct((TOPK, t), F32),
                   jax.ShapeDtypeStruct((TOPK, t), I32), jax.ShapeDtypeStruct((n_experts, 1), F32)],
        scratch_shapes=[pltpu.VMEM((n_experts, 1), F32)],
        compiler_params=_cparams("arbitrary"),
    )(logits, e_bias_col)


def _row_copy(src, src_row, dst, dst_row, sem):
    return pltpu.make_async_copy(src.at[pl.ds(src_row, 1), :], dst.at[pl.ds(dst_row, 1), :], sem)


def _experts_kernel(be_ref, tokc_ref, tokn_ref, dstp_ref, dstc_ref, h_hbm, wg_ref, wu_ref, wd_ref, y_hbm,
                    xbuf0, xbuf1, ybuf0, ybuf1, gsem, ssem, *, tm):
    i = pl.program_id(0)
    nblk = be_ref.shape[0]

    def gather_done(xb, sem):
        pltpu.make_async_copy(h_hbm.at[pl.ds(0, tm), :], xb, sem).wait()

    def scatter_done():
        pltpu.make_async_copy(ybuf0, y_hbm.at[pl.ds(0, tm), :], ssem.at[0]).wait()

    @pl.when(i == 0)
    def _():
        ybuf1[...] = jnp.zeros(ybuf1.shape, ybuf1.dtype)

        def body(r, carry):
            _row_copy(h_hbm, tokc_ref[0, 0, r], xbuf0, r, gsem.at[0]).start()
            return carry
        lax.fori_loop(0, tm, body, 0)

    @pl.when(i >= 1)
    def _():
        scatter_done()

    def step(x_cur, x_nxt, y_cur, y_prev, g_cur, g_nxt):
        gather_done(x_cur, g_cur)
        for r in range(tm):
            _row_copy(h_hbm, tokn_ref[0, 0, r], x_nxt, r, g_nxt).start(priority=r % 2)
            _row_copy(y_prev, r, y_hbm, dstp_ref[0, 0, r], ssem.at[0]).start(priority=(r + 1) % 2)
        x = _unpack_rows_bf16(x_cur[...])
        g = jnp.dot(x, wg_ref[0, 0].astype(BF16), preferred_element_type=F32)
        u = jnp.dot(x, wu_ref[0, 0].astype(BF16), preferred_element_type=F32)
        hid = (g * _sigmoid(g) * u).astype(BF16)
        y = jnp.dot(hid, wd_ref[0, 0].astype(BF16), preferred_element_type=F32)
        half = y.shape[1] // 2
        y_cur[...] = _pack_bf16_pair(y[:, :half], y[:, half:])

    @pl.when(i % 2 == 0)
    def _():
        step(xbuf0, xbuf1, ybuf0, ybuf1, gsem.at[0], gsem.at[1])

    @pl.when(i % 2 == 1)
    def _():
        step(xbuf1, xbuf0, ybuf1, ybuf0, gsem.at[1], gsem.at[0])

    last = (nblk - 1) % 2
    x_extra, g_extra, y_last = (xbuf1, gsem.at[1], ybuf0) if last == 0 else (xbuf0, gsem.at[0], ybuf1)

    @pl.when(i == nblk - 1)
    def _():
        gather_done(x_extra, g_extra)
        scatter_done()

        def body(r, carry):
            _row_copy(y_last, r, y_hbm, dstc_ref[0, 0, r], ssem.at[0]).start()
            return carry
        lax.fori_loop(0, tm, body, 0)
        scatter_done()


def _experts(h2, slot_tok, slot_dst, blk_expert, w_gate, w_up, w_down, layer, tm, y_rows):
    d, de = w_gate.shape[2], w_gate.shape[3]
    dp = h2.shape[1]
    nblk = blk_expert.shape[0]
    tok3 = slot_tok.reshape(nblk, 1, tm)
    dst3 = slot_dst.reshape(nblk + 1, 1, tm)
    smem_blk = lambda f: pl.BlockSpec((1, 1, tm), f, memory_space=pltpu.SMEM)
    grid_spec = pltpu.PrefetchScalarGridSpec(
        num_scalar_prefetch=1,
        grid=(nblk,),
        in_specs=[smem_blk(lambda i, be: (i, 0, 0)),
                  smem_blk(lambda i, be: (jnp.minimum(i + 1, nblk - 1), 0, 0)),
                  smem_blk(lambda i, be: (i, 0, 0)),
                  smem_blk(lambda i, be: (i + 1, 0, 0)),
                  pl.BlockSpec(memory_space=pl.ANY),
                  pl.BlockSpec((1, 1, d, de), lambda i, be: (layer, be[i], 0, 0)),
                  pl.BlockSpec((1, 1, d, de), lambda i, be: (layer, be[i], 0, 0)),
                  pl.BlockSpec((1, 1, de, d), lambda i, be: (layer, be[i], 0, 0))],
        out_specs=pl.BlockSpec(memory_space=pl.ANY),
        scratch_shapes=[pltpu.VMEM((tm, dp), jnp.uint32)] * 4
        + [pltpu.SemaphoreType.DMA((2,)), pltpu.SemaphoreType.DMA((1,))],
    )
    return pl.pallas_call(
        functools.partial(_experts_kernel, tm=tm),
        grid_spec=grid_spec,
        out_shape=jax.ShapeDtypeStruct((y_rows, dp), jnp.uint32),
        compiler_params=_cparams("arbitrary"),
    )(blk_expert, tok3, tok3, dst3, dst3, h2, w_gate, w_up, w_down)


def _combine_kernel(*refs, bt, rows, alpha):
    y_refs = refs[:TOPK]
    x1_ref, h_ref, gw_ref, g2_ref, wg_ref, wu_ref, wd_ref, lg_ref, lb_ref, o_ref = refs[TOPK:]
    x = _unpack_rows_bf16(h_ref[...])
    g = jnp.dot(x, wg_ref[...], preferred_element_type=F32)
    u = jnp.dot(x, wu_ref[...], preferred_element_type=F32)
    hid = (g * _sigmoid(g) * u).astype(BF16)
    moe = jnp.dot(hid, wd_ref[...], preferred_element_type=F32)
    half = moe.shape[1] // 2
    moe_lo, moe_hi = moe[:, :half], moe[:, half:]
    for j in range(TOPK):
        y_lo, y_hi = _unpack_bf16_pair(y_refs[j][...])
        w = gw_ref[:, j:j + 1]
        moe_lo = moe_lo + y_lo * w
        moe_hi = moe_hi + y_hi * w
    moe = jnp.concatenate([moe_lo, moe_hi], axis=1)
    y = alpha * x1_ref[...] + _per_batch(lambda m, g2: (1.0 + g2) * m, moe, (g2_ref,), bt, rows)
    o_ref[...] = _layer_norm(y, lg_ref[...], lb_ref[...])


def _combine(y_rows, gw, x1, h2, gate2, w_gate, w_up, w_down, ln_g, ln_b, rows_per_batch, alpha):
    t, d = x1.shape
    dp = h2.shape[1]
    tt, bt, vec_idx = _batch_tiling(t, rows_per_batch, 128)
    nt = t // tt
    row = pl.BlockSpec((1, d), lambda i: (0, 0))
    tile = pl.BlockSpec((tt, d), lambda i: (i, 0))
    packed_tile = pl.BlockSpec((tt, dp), lambda i: (i, 0))
    full = lambda a: pl.BlockSpec(a.shape, lambda i: (0, 0))
    slab = lambda j: pl.BlockSpec((tt, dp), lambda i: (j * nt + i, 0))
    return pl.pallas_call(
        functools.partial(_combine_kernel, bt=bt, rows=rows_per_batch, alpha=alpha),
        grid=(nt,),
        in_specs=[slab(j) for j in range(TOPK)] + [
            tile, packed_tile, pl.BlockSpec((tt, TOPK), lambda i: (i, 0)),
            pl.BlockSpec((bt, 1, d), lambda i: (vec_idx(i), 0, 0)),
            full(w_gate), full(w_up), full(w_down), row, row],
        out_specs=tile,
        out_shape=jax.ShapeDtypeStruct((t, d), F32),
        compiler_params=_cparams("arbitrary"),
    )(*([y_rows] * TOPK), x1, h2, gw, gate2, w_gate, w_up, w_down, ln_g, ln_b)


def _dispatch_plan(eidx_t, rank_t, counts, tm):
    k, t = eidx_t.shape
    n_experts = counts.shape[0]
    nblk = (t * k + tm - 1) // tm + n_experts
    padded = (counts + tm - 1) // tm * tm
    pend = jnp.cumsum(padded)
    pstart = pend - padded
    experts = jnp.arange(n_experts, dtype=I32)
    start_of = jnp.sum(jnp.where(eidx_t[:, :, None] == experts, pstart, 0), axis=-1)
    pos = (start_of + rank_t).reshape(-1)
    flat = jnp.full((nblk * tm,), -1, I32).at[pos].set(jnp.arange(k * t, dtype=I32), unique_indices=True)
    real = flat >= 0
    slot_tok = jnp.where(real, flat % t, 0)
    dump = k * t + jnp.arange(nblk * tm, dtype=I32) % tm
    slot_dst = jnp.concatenate([dump[:tm], jnp.where(real, flat, dump)])
    blk_row = jnp.arange(nblk, dtype=I32) * tm
    blk_expert = jnp.minimum(jnp.sum((pend[None, :] <= blk_row[:, None]).astype(I32), axis=1), n_experts - 1)
    return slot_tok, slot_dst, blk_expert


def _layer(x, mods, layer, wts, rows_per_batch, alpha, mix_state):
    b, s, d = x.shape
    t = b * s
    sh1, sc1, g1, sh2, sc2, g2 = mods
    x2 = x.reshape(t, d)
    n_heads = wts['b_f'].shape[1]
    da = n_heads * HEAD_DIM
    dc = d - da
    assert da == dc, "column-block indexing of the projection assumes equal attention / conv widths"
    cb = wts['conv_w'].shape[1] - 1

    p, flog = _in_proj(x2, sh1, sc1, wts['w_in_main'][layer], wts['w_in_f'][layer], rows_per_batch)
    p3 = p.reshape(b, s, p.shape[1])
    k_new = p3[:, :, da:2 * da]
    v_new = p3[:, :, 2 * da:3 * da]
    b_f_row = wts['b_f'][layer][None, :]
    conv_args = (wts['conv_w'][layer], wts['conv_b'][layer][None, :], wts['conv_ln_g'][layer][None, :],
                 wts['conv_ln_b'][layer][None, :])

    if mix_state is None:
        logf, _, aug_q, aug_k = _forget_cumsum(flog.reshape(b, s, n_heads), b_f_row,
                                               jnp.zeros((b, 1, n_heads), F32), True)
        attn = _attention(p3, 0, p3, 1, p3, 2, aug_q, aug_k, 0)
        conv, tail = _conv_module(p3, 3, p3, 3, 4, False, *conv_args)
    else:
        ck_all, cv_all, clf, cconv = mix_state
        past = ck_all.shape[1]
        _, cum_c, _, aug_kc = _forget_cumsum(clf, b_f_row, jnp.zeros((b, 1, n_heads), F32), False)
        logf, _, aug_q, aug_kn = _forget_cumsum(flog.reshape(b, s, n_heads), b_f_row,
                                                cum_c[:, past - 1:past, :], True)
        attn = _attention(p3, 0, ck_all, 0, cv_all, 0, aug_q, aug_kc, past, kv_batch_off=layer * b,
                          tail=(p3, 1, p3, 2, aug_kn))
        hist = jnp.pad(cconv, ((0, 0), (CONV_HALO - cb, 0), (0, 0)))
        conv, tail = _conv_module(p3, 3, hist, 0, 0, True, *conv_args)

    x1, h, logits = _out_proj(attn.reshape(t, da), conv.reshape(t, dc), x2, g1, sh2, sc2,
                              wts['w_out_a'][layer], wts['w_out_c'][layer],
                              wts['ln1_g'][layer][None, :], wts['ln1_b'][layer][None, :],
                              wts['wr_hi'][layer], wts['wr_lo'][layer], rows_per_batch, alpha)

    eidx_t, gw_t, rank_t, counts = _route(logits, wts['e_bias'][layer][:, None])
    tm_e = _tile(t, 256)
    slot_tok, slot_dst, blk_expert = _dispatch_plan(eidx_t, rank_t, counts[:, 0].astype(I32), tm_e)
    y_rows = _experts(h, slot_tok, slot_dst, blk_expert, wts['w_e_gate'], wts['w_e_up'], wts['w_e_down'],
                      layer, tm_e, TOPK * t + tm_e)
    x_out = _combine(y_rows, gw_t.T, x1, h, g2, wts['w_s_gate'][layer], wts['w_s_up'][layer],
                     wts['w_s_down'][layer], wts['ln2_g'][layer][None, :], wts['ln2_b'][layer][None, :],
                     rows_per_batch, alpha)

    state = (k_new.reshape(b, s, n_heads, HEAD_DIM), v_new.reshape(b, s, n_heads, HEAD_DIM), logf,
             tail[:, CONV_HALO - cb:, :])
    return x_out.reshape(b, s, d), state


def kernel(x_prompt, x_sample, c_prompt, c_sample, cache_k, cache_v, cache_logf, state_conv, w_ada, b_ada, w_in, b_f, conv_w, conv_b, conv_ln_g, conv_ln_b, w_out, ln1_g, ln1_b, w_router, e_bias, w_e_gate, w_e_up, w_e_down, w_s_gate, w_s_up, w_s_down, ln2_g, ln2_b):
    depth, d, _ = w_in.shape
    n_heads = b_f.shape[1]
    da = n_heads * HEAD_DIM
    n_experts = w_router.shape[2]
    alpha = (2 * depth) ** 0.25
    bp = x_prompt.shape[0]

    fo = 3 * da
    wr_pad = jnp.pad(w_router, ((0, 0), (0, 0), (0, ROUTER_LANES - n_experts)))
    wr_hi = wr_pad.astype(BF16)
    wts = dict(
        w_in_main=jnp.concatenate([w_in[:, :, :fo], w_in[:, :, fo + n_heads:]], axis=2).astype(BF16),
        w_in_f=w_in[:, :, fo:fo + n_heads].astype(BF16),
        b_f=b_f, conv_w=conv_w, conv_b=conv_b, conv_ln_g=conv_ln_g, conv_ln_b=conv_ln_b,
        w_out_a=w_out[:, :da, :].astype(BF16), w_out_c=w_out[:, da:, :].astype(BF16),
        ln1_g=ln1_g, ln1_b=ln1_b, ln2_g=ln2_g, ln2_b=ln2_b,
        wr_hi=wr_hi, wr_lo=(wr_pad - wr_hi.astype(F32)).astype(BF16), e_bias=e_bias,
        w_e_gate=w_e_gate, w_e_up=w_e_up, w_e_down=w_e_down,
        w_s_gate=w_s_gate.astype(BF16), w_s_up=w_s_up.astype(BF16), w_s_down=w_s_down.astype(BF16),
    )

    mod_all = _adaln(jnp.concatenate([c_prompt, c_sample], axis=0), w_ada, b_ada)

    def mods(layer, lo, hi):
        m = mod_all[layer, lo:hi]
        return tuple(m[:, None, i * d:(i + 1) * d] for i in range(6))

    xp, xs = x_prompt, x_sample
    outs_p, outs_s = [], []
    for layer in range(depth):
        xp, st = _layer(xp, mods(layer, 0, bp), layer, wts, x_prompt.shape[1], alpha, None)
        outs_p.append(st)
        cache = (cache_k.reshape(-1, cache_k.shape[2], da), cache_v.reshape(-1, cache_v.shape[2], da),
                 cache_logf[layer], state_conv[layer])
        xs, st = _layer(xs, mods(layer, bp, bp + x_sample.shape[0]), layer, wts, x_sample.shape[1], alpha, cache)
        outs_s.append(st)

    stack = lambda outs, i: jnp.stack([o[i] for o in outs])
    return (xp, xs, stack(outs_p, 0), stack(outs_p, 1), stack(outs_p, 2), stack(outs_p, 3),
            stack(outs_s, 0), stack(outs_s, 1), stack(outs_s, 2), stack(outs_s, 3))
```

```python
import functools

import jax
import jax.numpy as jnp
from jax import lax
from jax.experimental import pallas as pl
from jax.experimental.pallas import tpu as pltpu

F32 = jnp.float32
BF16 = jnp.bfloat16
I32 = jnp.int32

HEAD_DIM = 128
N_GROUPS = 8
TOPK_GROUPS = 4
TOPK = 8
ROUTED_SCALE = 2.5
LN_EPS = 1e-5
NEG_BIG = -1e30

CONV_HALO = 32
ROUTER_LANES = 128
VMEM_LIMIT = 56 * 1024 * 1024


def _cparams(*sem):
    return pltpu.CompilerParams(dimension_semantics=sem, vmem_limit_bytes=VMEM_LIMIT)


def _tile(n, target, mult=8):
    if n <= target:
        return n
    t = (target // mult) * mult
    while t >= mult:
        if n % t == 0:
            return t
        t -= mult
    return n


def _sigmoid(x):
    return 1.0 / (1.0 + jnp.exp(-x))


def _log_sigmoid(x):
    return jnp.minimum(x, 0.0) - jnp.log(1.0 + jnp.exp(-jnp.abs(x)))


def _layer_norm(y, g, b):
    mu = jnp.mean(y, axis=-1, keepdims=True)
    d = y - mu
    var = jnp.mean(d * d, axis=-1, keepdims=True)
    return d * lax.rsqrt(var + LN_EPS) * g + b


def _pack_bf16_pair(lo, hi):
    lo_bits = lax.bitcast_convert_type(lo.astype(BF16).astype(F32), jnp.uint32)
    hi_bits = lax.bitcast_convert_type(hi.astype(BF16).astype(F32), jnp.uint32)
    return (lo_bits >> 16) | (hi_bits & jnp.uint32(0xFFFF0000))


def _unpack_bf16_pair(w):
    lo = lax.bitcast_convert_type(w << 16, F32)
    hi = lax.bitcast_convert_type(w & jnp.uint32(0xFFFF0000), F32)
    return lo, hi


def _unpack_rows_bf16(w):
    lo, hi = _unpack_bf16_pair(w)
    return jnp.concatenate([lo.astype(BF16), hi.astype(BF16)], axis=1)


def _per_batch(fn, x, refs, bt, rows):
    if bt == 1:
        return fn(x, *[r[0] for r in refs])
    tm, d = x.shape
    out = fn(x.reshape(bt, rows, d), *[r[...] for r in refs])
    return out.reshape(tm, d)


def _batch_tiling(n_rows, rows_per_batch, target):
    if rows_per_batch >= target:
        tm = _tile(rows_per_batch, target)
        per = rows_per_batch // tm
        return tm, 1, (lambda i: i // per)
    nb = n_rows // rows_per_batch
    bt = _tile(nb, max(1, target // rows_per_batch), 1)
    return bt * rows_per_batch, bt, (lambda i: i)


def _adaln_kernel(c_ref, w_ref, b_ref, o_ref):
    c = c_ref[...]
    a = (c * _sigmoid(c)).astype(BF16)
    o_ref[0] = jnp.dot(a, w_ref[0].astype(BF16), preferred_element_type=F32) + b_ref[0]


def _adaln(c_all, w_ada, b_ada):
    depth, d, n6 = w_ada.shape
    bc = c_all.shape[0]
    tn = _tile(n6, 1024, 128)
    return pl.pallas_call(
        _adaln_kernel,
        grid=(depth, n6 // tn),
        in_specs=[pl.BlockSpec((bc, d), lambda l, j: (0, 0)),
                  pl.BlockSpec((1, d, tn), lambda l, j: (l, 0, j)),
                  pl.BlockSpec((1, 1, tn), lambda l, j: (l, 0, j))],
        out_specs=pl.BlockSpec((1, bc, tn), lambda l, j: (l, 0, j)),
        out_shape=jax.ShapeDtypeStruct((depth, bc, n6), F32),
        compiler_params=_cparams("arbitrary", "arbitrary"),
    )(c_all, w_ada, b_ada.reshape(depth, 1, n6))


def _in_proj_kernel(x_ref, sh_ref, sc_ref, w_ref, wf_ref, p_ref, f_ref, u_ref, *, bt, rows):
    @pl.when(pl.program_id(1) == 0)
    def _():
        u = _per_batch(lambda x, sh, sc: x * (1.0 + sc) + sh, x_ref[...], (sh_ref, sc_ref), bt, rows)
        ub = u.astype(BF16)
        u_ref[...] = ub
        f_ref[...] = jnp.dot(ub, wf_ref[...], preferred_element_type=F32)

    p_ref[...] = jnp.dot(u_ref[...], w_ref[...], preferred_element_type=F32)


def _in_proj(x2, shift, scale, w_main, w_f, rows_per_batch):
    t, d = x2.shape
    n_main = w_main.shape[1]
    h = w_f.shape[1]
    tm, bt, vec_idx = _batch_tiling(t, rows_per_batch, 1024)
    tn = _tile(n_main, 512, 128)
    vec_spec = pl.BlockSpec((bt, 1, d), lambda i, j: (vec_idx(i), 0, 0))
    return pl.pallas_call(
        functools.partial(_in_proj_kernel, bt=bt, rows=rows_per_batch),
        grid=(t // tm, n_main // tn),
        in_specs=[pl.BlockSpec((tm, d), lambda i, j: (i, 0)), vec_spec, vec_spec,
                  pl.BlockSpec((d, tn), lambda i, j: (0, j)),
                  pl.BlockSpec((d, h), lambda i, j: (0, 0))],
        out_specs=[pl.BlockSpec((tm, tn), lambda i, j: (i, j)),
                   pl.BlockSpec((tm, h), lambda i, j: (i, 0))],
        out_shape=[jax.ShapeDtypeStruct((t, n_main), F32), jax.ShapeDtypeStruct((t, h), F32)],
        scratch_shapes=[pltpu.VMEM((tm, d), BF16)],
        compiler_params=_cparams("arbitrary", "arbitrary"),
    )(x2, shift, scale, w_main, w_f)


LOG2E = 1.4426950408889634
AUG_TERMS = 3


def _bias_placement(n_heads):
    rows = (AUG_TERMS + 1) * n_heads
    eq = [[0.0] * (n_heads * HEAD_DIM) for _ in range(rows)]
    ek = [[0.0] * (n_heads * HEAD_DIM) for _ in range(rows)]
    for h in range(n_heads):
        for n in range(AUG_TERMS):
            eq[n * n_heads + h][h * HEAD_DIM + n] = 1.0
            ek[n * n_heads + h][h * HEAD_DIM + AUG_TERMS + n] = -1.0
            eq[AUG_TERMS * n_heads + h][h * HEAD_DIM + AUG_TERMS + n] = 1.0
            ek[AUG_TERMS * n_heads + h][h * HEAD_DIM + n] = 1.0
    return jnp.array(eq, BF16), jnp.array(ek, BF16)


def _cumsum_kernel(z_ref, bf_ref, c0_ref, eq_ref, ek_ref, lf_ref, cum_ref, aq_ref, ak_ref, *, logsig, chunk):
    s, n_heads = z_ref.shape[1], z_ref.shape[2]
    r = lax.broadcasted_iota(I32, (chunk, chunk), 0)
    c = lax.broadcasted_iota(I32, (chunk, chunk), 1)
    tri = (r >= c).astype(F32)
    carry = c0_ref[0]
    for ci in range(s // chunk):
        sl = slice(ci * chunk, (ci + 1) * chunk)
        z = z_ref[0, sl, :]
        lf = _log_sigmoid(z + bf_ref[...]) if logsig else z
        lf_ref[0, sl, :] = lf
        cum = jnp.dot(tri, lf, precision=lax.Precision.HIGHEST, preferred_element_type=F32) + carry
        cum_ref[0, sl, :] = cum
        carry = cum[chunk - 1:chunk, :]
        rem = cum * LOG2E
        pieces = []
        for _ in range(AUG_TERMS):
            piece = rem.astype(BF16).astype(F32)
            pieces.append(piece)
            rem = rem - piece
        pm = jnp.concatenate(pieces + [jnp.ones((chunk, n_heads), F32)], axis=1).astype(BF16)
        aq_ref[0, sl, :] = jnp.dot(pm, eq_ref[...], preferred_element_type=F32).astype(BF16)
        ak_ref[0, sl, :] = jnp.dot(pm, ek_ref[...], preferred_element_type=F32).astype(BF16)


def _forget_cumsum(z, b_f_row, carry0, logsig):
    b, s, h = z.shape
    chunk = _tile(s, 256)
    blk = pl.BlockSpec((1, s, h), lambda i: (i, 0, 0))
    aug = pl.BlockSpec((1, s, h * HEAD_DIM), lambda i: (i, 0, 0))
    eq, ek = _bias_placement(h)
    place = pl.BlockSpec(eq.shape, lambda i: (0, 0))
    return pl.pallas_call(
        functools.partial(_cumsum_kernel, logsig=logsig, chunk=chunk),
        grid=(b,),
        in_specs=[blk, pl.BlockSpec((1, h), lambda i: (0, 0)), pl.BlockSpec((1, 1, h), lambda i: (i, 0, 0)),
                  place, place],
        out_specs=[blk, blk, aug, aug],
        out_shape=[jax.ShapeDtypeStruct((b, s, h), F32)] * 2
        + [jax.ShapeDtypeStruct((b, s, h * HEAD_DIM), BF16)] * 2,
        compiler_params=_cparams("arbitrary"),
    )(z, b_f_row, carry0, eq, ek)


def _attn_kernel(*refs, n_heads, tq, tk, q_off, scale, tail_rows):
    if tail_rows:
        q_ref, k_ref, v_ref, aq_ref, ak_ref, kt_ref, vt_ref, akt_ref, o_ref, m_ref, acc_ref = refs
    else:
        q_ref, k_ref, v_ref, aq_ref, ak_ref, o_ref, m_ref, acc_ref = refs
    qi = pl.program_id(1)
    ki = pl.program_id(2)
    n_main = pl.num_programs(2) - (1 if tail_rows else 0)

    @pl.when(ki == 0)
    def _():
        m_ref[...] = jnp.full(m_ref.shape, NEG_BIG, F32)
        acc_ref[...] = jnp.zeros(acc_ref.shape, F32)

    q_start = qi * tq + q_off

    def attend(kr, vr, akr, k_start, width):
        qpos = q_start + lax.broadcasted_iota(I32, (tq, width), 0)
        kpos = k_start + lax.broadcasted_iota(I32, (tq, width), 1)
        visible = kpos <= qpos
        ones = jnp.ones((width, HEAD_DIM), BF16)
        for h in range(n_heads):
            hs = slice(h * HEAD_DIM, (h + 1) * HEAD_DIM)
            qa = jnp.concatenate([(q_ref[0, :, hs] * (scale * LOG2E)).astype(BF16), aq_ref[0, :, hs]], axis=1)
            ka = jnp.concatenate([kr[0, :, hs].astype(BF16), akr[0, :, hs]], axis=1)
            va = jnp.concatenate([vr[0, :, hs].astype(BF16), ones], axis=1)
            s = lax.dot_general(qa, ka, (((1,), (1,)), ((), ())), preferred_element_type=F32)
            s = jnp.where(visible, s, NEG_BIG)
            m_prev = m_ref[h]
            m_new = jnp.maximum(m_prev, jnp.max(s, axis=-1, keepdims=True))
            p = jnp.exp2(s - m_new).astype(BF16)
            acc_ref[h] = jnp.exp2(m_prev - m_new) * acc_ref[h] + jnp.dot(p, va, preferred_element_type=F32)
            m_ref[h] = m_new

    @pl.when((ki < n_main) & (ki * tk <= q_start + (tq - 1)))
    def _():
        attend(k_ref, v_ref, ak_ref, ki * tk, tk)

    if tail_rows:
        @pl.when(ki == n_main)
        def _():
            attend(kt_ref, vt_ref, akt_ref, n_main * tk, tail_rows)

    @pl.when(ki == pl.num_programs(2) - 1)
    def _():
        for h in range(n_heads):
            acc = acc_ref[h]
            o_ref[0, :, h * HEAD_DIM:(h + 1) * HEAD_DIM] = (
                acc[:, :HEAD_DIM] / acc[:, HEAD_DIM:HEAD_DIM + 1]).astype(o_ref.dtype)


def _attention(q_arr, q_col, k_arr, k_col, v_arr, v_col, aug_q, aug_k, q_off, kv_batch_off=0, tail=None):
    b, sq, da = aug_q.shape
    sk = aug_k.shape[1]
    n_heads = da // HEAD_DIM
    tq = _tile(sq, 512)
    tk = _tile(sk, 512)
    n_main = sk // tk

    def k_blk(i, j):
        return jnp.minimum(jnp.minimum(j, (i * tq + q_off + tq - 1) // tk), n_main - 1)

    in_specs = [pl.BlockSpec((1, tq, da), lambda bi, i, j: (bi, i, q_col)),
                pl.BlockSpec((1, tk, da), lambda bi, i, j: (kv_batch_off + bi, k_blk(i, j), k_col)),
                pl.BlockSpec((1, tk, da), lambda bi, i, j: (kv_batch_off + bi, k_blk(i, j), v_col)),
                pl.BlockSpec((1, tq, da), lambda bi, i, j: (bi, i, 0)),
                pl.BlockSpec((1, tk, da), lambda bi, i, j: (bi, k_blk(i, j), 0))]
    args = [q_arr, k_arr, v_arr, aug_q, aug_k]
    tail_rows = 0
    if tail is not None:
        tk_arr, tk_col, tv_arr, tv_col, t_aug = tail
        tail_rows = t_aug.shape[1]
        in_specs += [pl.BlockSpec((1, tail_rows, da), lambda bi, i, j: (bi, 0, tk_col)),
                     pl.BlockSpec((1, tail_rows, da), lambda bi, i, j: (bi, 0, tv_col)),
                     pl.BlockSpec((1, tail_rows, da), lambda bi, i, j: (bi, 0, 0))]
        args += [tk_arr, tv_arr, t_aug]

    kern = functools.partial(_attn_kernel, n_heads=n_heads, tq=tq, tk=tk, q_off=q_off, scale=HEAD_DIM ** -0.5,
                             tail_rows=tail_rows)
    return pl.pallas_call(
        kern,
        grid=(b, sq // tq, n_main + (1 if tail_rows else 0)),
        in_specs=in_specs,
        out_specs=pl.BlockSpec((1, tq, da), lambda bi, i, j: (bi, i, 0)),
        out_shape=jax.ShapeDtypeStruct((b, sq, da), BF16),
        scratch_shapes=[pltpu.VMEM((n_heads, tq, 1), F32), pltpu.VMEM((n_heads, tq, 2 * HEAD_DIM), F32)],
        compiler_params=_cparams("arbitrary", "arbitrary", "arbitrary"),
    )(*args)


SUBLANES = 8


def _conv_kernel(a_ref, b_ref, ha_ref, hb_ref, w_ref, cb_ref, g_ref, be_ref, o_ref, tail_ref, xbuf, xsh, ybuf,
                 *, tt, kw, halo_is_glu, zero_first):
    glu = a_ref[0] * _sigmoid(b_ref[0])
    xbuf[CONV_HALO:CONV_HALO + tt, :] = glu
    halo = ha_ref[0] if halo_is_glu else ha_ref[0] * _sigmoid(hb_ref[0])
    if zero_first:
        halo = jnp.where(pl.program_id(1) == 0, 0.0, halo)
    xbuf[0:CONV_HALO, :] = halo
    tail_ref[0] = xbuf[tt:tt + CONV_HALO, :]

    span = tt + CONV_HALO - SUBLANES
    for s in range(1, SUBLANES):
        xsh[s - 1, 0:span, :] = xbuf[s:s + span, :]

    channels = xbuf.shape[1]
    rc = _tile(tt, 64)
    first = CONV_HALO - (kw - 1)
    for c0 in range(0, channels, 128):
        cs = slice(c0, c0 + 128)
        for r0 in range(0, tt, rc):
            acc = jnp.zeros((rc, 128), F32)
            for j in range(kw):
                shift = (first + j) % SUBLANES
                base = (first + j) - shift + r0
                src = xbuf if shift == 0 else xsh.at[shift - 1]
                acc = acc + src[base:base + rc, cs] * w_ref[j:j + 1, cs]
            ybuf[r0:r0 + rc, cs] = acc
    y = _layer_norm(ybuf[...] + cb_ref[...], g_ref[...], be_ref[...])
    o_ref[0] = (y * _sigmoid(y)).astype(o_ref.dtype)


def _conv_module(p3, a_col, halo_arr, halo_a_col, halo_b_col, halo_is_glu, conv_w, conv_b, ln_g, ln_b):
    b, s, _ = p3.shape
    kw, c = conv_w.shape
    assert kw - 1 <= CONV_HALO
    tt = _tile(s, 128)
    per = tt // CONV_HALO if tt >= CONV_HALO else 1

    def halo_row(i):
        return jnp.maximum(i * per - 1, 0)

    kern = functools.partial(_conv_kernel, tt=tt, kw=kw, halo_is_glu=halo_is_glu, zero_first=not halo_is_glu)
    vec = pl.BlockSpec((1, c), lambda bi, i: (0, 0))
    return pl.pallas_call(
        kern,
        grid=(b, s // tt),
        in_specs=[pl.BlockSpec((1, tt, c), lambda bi, i: (bi, i, a_col)),
                  pl.BlockSpec((1, tt, c), lambda bi, i: (bi, i, a_col + 1)),
                  pl.BlockSpec((1, CONV_HALO, c), lambda bi, i: (bi, halo_row(i), halo_a_col)),
                  pl.BlockSpec((1, CONV_HALO, c), lambda bi, i: (bi, halo_row(i), halo_b_col)),
                  pl.BlockSpec((kw, c), lambda bi, i: (0, 0)), vec, vec, vec],
        out_specs=[pl.BlockSpec((1, tt, c), lambda bi, i: (bi, i, 0)),
                   pl.BlockSpec((1, CONV_HALO, c), lambda bi, i: (bi, 0, 0))],
        out_shape=[jax.ShapeDtypeStruct((b, s, c), BF16), jax.ShapeDtypeStruct((b, CONV_HALO, c), F32)],
        scratch_shapes=[pltpu.VMEM((CONV_HALO + tt, c), F32), pltpu.VMEM((SUBLANES - 1, CONV_HALO + tt, c), F32),
                        pltpu.VMEM((tt, c), F32)],
        compiler_params=_cparams("arbitrary", "arbitrary"),
    )(p3, p3, halo_arr, halo_arr, conv_w, conv_b, ln_g, ln_b)


def _out_proj_kernel(at_ref, cv_ref, x_ref, g1_ref, sh2_ref, sc2_ref, wa_ref, wc_ref, lg_ref, lb_ref,
                     wrh_ref, wrl_ref, x1_ref, h_ref, lo_ref, *, bt, rows, alpha):
    mix = (jnp.dot(at_ref[...], wa_ref[...], preferred_element_type=F32)
           + jnp.dot(cv_ref[...], wc_ref[...], preferred_element_type=F32))
    y = alpha * x_ref[...] + _per_batch(lambda m, g: (1.0 + g) * m, mix, (g1_ref,), bt, rows)
    x1 = _layer_norm(y, lg_ref[...], lb_ref[...])
    x1_ref[...] = x1
    h = _per_batch(lambda x, sh, sc: x * (1.0 + sc) + sh, x1, (sh2_ref, sc2_ref), bt, rows)
    half = h.shape[1] // 2
    h_ref[...] = _pack_bf16_pair(h[:, :half], h[:, half:])
    hh = h.astype(BF16)
    hl = (h - hh.astype(F32)).astype(BF16)
    lo_ref[...] = (jnp.dot(hh, wrh_ref[...], preferred_element_type=F32)
                   + jnp.dot(hh, wrl_ref[...], preferred_element_type=F32)
                   + jnp.dot(hl, wrh_ref[...], preferred_element_type=F32))


def _out_proj(attn2, conv2, x2, gate1, shift2, scale2, w_out_a, w_out_c, ln_g, ln_b, wr_hi, wr_lo,
              rows_per_batch, alpha):
    t, d = x2.shape
    da = attn2.shape[1]
    dc = conv2.shape[1]
    tm, bt, vec_idx = _batch_tiling(t, rows_per_batch, 256)
    vec_spec = pl.BlockSpec((bt, 1, d), lambda i: (vec_idx(i), 0, 0))
    row = pl.BlockSpec((1, d), lambda i: (0, 0))
    full = lambda a: pl.BlockSpec(a.shape, lambda i: (0, 0))
    return pl.pallas_call(
        functools.partial(_out_proj_kernel, bt=bt, rows=rows_per_batch, alpha=alpha),
        grid=(t // tm,),
        in_specs=[pl.BlockSpec((tm, da), lambda i: (i, 0)), pl.BlockSpec((tm, dc), lambda i: (i, 0)),
                  pl.BlockSpec((tm, d), lambda i: (i, 0)), vec_spec, vec_spec, vec_spec,
                  full(w_out_a), full(w_out_c), row, row, full(wr_hi), full(wr_lo)],
        out_specs=[pl.BlockSpec((tm, d), lambda i: (i, 0)), pl.BlockSpec((tm, d // 2), lambda i: (i, 0)),
                   pl.BlockSpec((tm, ROUTER_LANES), lambda i: (i, 0))],
        out_shape=[jax.ShapeDtypeStruct((t, d), F32), jax.ShapeDtypeStruct((t, d // 2), jnp.uint32),
                   jax.ShapeDtypeStruct((t, ROUTER_LANES), F32)],
        compiler_params=_cparams("arbitrary"),
    )(attn2, conv2, x2, gate1, shift2, scale2, w_out_a, w_out_c, ln_g, ln_b, wr_hi, wr_lo)


def _route_kernel(lo_ref, eb_ref, eidx_ref, gw_ref, rank_ref, cnt_ref, carry_ref, *, n_experts, tr):
    i = pl.program_id(0)
    gsz = n_experts // N_GROUPS

    @pl.when(i == 0)
    def _():
        carry_ref[...] = jnp.zeros(carry_ref.shape, F32)

    logits = lo_ref[...].T[0:n_experts, :]
    sub = lax.broadcasted_iota(I32, (gsz, tr), 0)
    s_g, sb_g, gid_g = [], [], []
    for g in range(N_GROUPS):
        rs = slice(g * gsz, (g + 1) * gsz)
        s = _sigmoid(logits[rs, :])
        s_g.append(s)
        sb_g.append(s + eb_ref[rs, :])
        gid_g.append(sub + g * gsz)

    gscore = []
    for g in range(N_GROUPS):
        x = sb_g[g]
        m1 = jnp.max(x, axis=0, keepdims=True)
        i1 = jnp.min(jnp.where(x == m1, sub, gsz), axis=0, keepdims=True)
        m2 = jnp.max(jnp.where(sub == i1, -jnp.inf, x), axis=0, keepdims=True)
        gscore.append(m1 + m2)
    gs = jnp.concatenate(gscore, axis=0)
    gi = lax.broadcasted_iota(I32, (N_GROUPS, tr), 0)
    sel = jnp.zeros((N_GROUPS, tr), F32)
    for _ in range(TOPK_GROUPS):
        gm = jnp.max(gs, axis=0, keepdims=True)
        pick = gi == jnp.min(jnp.where(gs == gm, gi, N_GROUPS), axis=0, keepdims=True)
        sel = jnp.where(pick, 1.0, sel)
        gs = jnp.where(pick, -jnp.inf, gs)

    cand = [jnp.where(sel[g:g + 1, :] > 0.0, sb_g[g], -jnp.inf) for g in range(N_GROUPS)]
    member = [jnp.zeros((gsz, tr), F32) for _ in range(N_GROUPS)]
    eidx, wsel = [], []
    for _ in range(TOPK):
        mx = cand[0]
        for g in range(1, N_GROUPS):
            mx = jnp.maximum(mx, cand[g])
        mx = jnp.max(mx, axis=0, keepdims=True)
        ei = jnp.where(cand[0] == mx, gid_g[0], n_experts)
        for g in range(1, N_GROUPS):
            ei = jnp.minimum(ei, jnp.where(cand[g] == mx, gid_g[g], n_experts))
        ei = jnp.min(ei, axis=0, keepdims=True)
        w = jnp.zeros((gsz, tr), F32)
        for g in range(N_GROUPS):
            hit = gid_g[g] == ei
            w = w + jnp.where(hit, s_g[g], 0.0)
            member[g] = jnp.where(hit, 1.0, member[g])
            cand[g] = jnp.where(hit, -jnp.inf, cand[g])
        eidx.append(ei)
        wsel.append(jnp.sum(w, axis=0, keepdims=True))
    w8 = jnp.concatenate(wsel, axis=0)
    gw_ref[...] = w8 / jnp.sum(w8, axis=0, keepdims=True) * ROUTED_SCALE
    eidx_ref[...] = jnp.concatenate(eidx, axis=0)

    mt = jnp.concatenate(member, axis=0)
    r = lax.broadcasted_iota(I32, (tr, tr), 0)
    c = lax.broadcasted_iota(I32, (tr, tr), 1)
    before = (r < c).astype(BF16)
    prior = jnp.dot(mt.astype(BF16), before, preferred_element_type=F32) + carry_ref[...]
    eio = lax.broadcasted_iota(I32, (n_experts, tr), 0)
    ranks = [jnp.sum(jnp.where(eio == ei, prior, 0.0), axis=0, keepdims=True) for ei in eidx]
    rank_ref[...] = jnp.concatenate(ranks, axis=0).astype(I32)
    carry_ref[...] = carry_ref[...] + jnp.sum(mt, axis=1, keepdims=True)
    cnt_ref[...] = carry_ref[...]


def _route(logits, e_bias_col):
    t = logits.shape[0]
    n_experts = e_bias_col.shape[0]
    tr = _tile(t, 512, 128)
    out_blk = pl.BlockSpec((TOPK, tr), lambda i: (0, i))
    return pl.pallas_call(
        functools.partial(_route_kernel, n_experts=n_experts, tr=tr),
        grid=(t // tr,),
        in_specs=[pl.BlockSpec((tr, ROUTER_LANES), lambda i: (i, 0)),
                  pl.BlockSpec((n_experts, 1), lambda i: (0, 0))],
        out_specs=[out_blk, out_blk, out_blk, pl.BlockSpec((n_experts, 1), lambda i: (0, 0))],
        out_shape=[jax.ShapeDtypeStruct((TOPK, t), I32), jax.ShapeDtypeStruct((TOPK, t), F32),
                   jax.ShapeDtypeStruct((TOPK, t), I32), jax.ShapeDtypeStruct((n_experts, 1), F32)],
        scratch_shapes=[pltpu.VMEM((n_experts, 1), F32)],
        compiler_params=_cparams("arbitrary"),
    )(logits, e_bias_col)


def _row_copy(src, src_row, dst, dst_row, sem):
    return pltpu.make_async_copy(src.at[pl.ds(src_row, 1), :], dst.at[pl.ds(dst_row, 1), :], sem)


def _experts_kernel(be_ref, tokc_ref, tokn_ref, dstp_ref, dstc_ref, h_hbm, wg_ref, wu_ref, wd_ref, y_hbm,
                    xbuf0, xbuf1, ybuf0, ybuf1, gsem, ssem, *, tm):
    i = pl.program_id(0)
    nblk = be_ref.shape[0]

    def gather_done(xb, sem):
        pltpu.make_async_copy(h_hbm.at[pl.ds(0, tm), :], xb, sem).wait()

    def scatter_done():
        pltpu.make_async_copy(ybuf0, y_hbm.at[pl.ds(0, tm), :], ssem.at[0]).wait()

    @pl.when(i == 0)
    def _():
        ybuf1[...] = jnp.zeros(ybuf1.shape, ybuf1.dtype)

        def body(r, carry):
            _row_copy(h_hbm, tokc_ref[0, 0, r], xbuf0, r, gsem.at[0]).start()
            return carry
        lax.fori_loop(0, tm, body, 0)

    @pl.when(i >= 1)
    def _():
        scatter_done()

    def issue(x_nxt, y_prev, g_nxt):
        for r in range(tm):
            _row_copy(h_hbm, tokn_ref[0, 0, r], x_nxt, r, g_nxt).start(priority=r % 2)
            _row_copy(y_prev, r, y_hbm, dstp_ref[0, 0, r], ssem.at[0]).start(priority=(r + 1) % 2)

    def compute(x_cur, y_cur, g_cur):
        gather_done(x_cur, g_cur)
        x = _unpack_rows_bf16(x_cur[...])
        g = jnp.dot(x, wg_ref[0, 0].astype(BF16), preferred_element_type=F32)
        u = jnp.dot(x, wu_ref[0, 0].astype(BF16), preferred_element_type=F32)
        hid = (g * _sigmoid(g) * u).astype(BF16)
        y = jnp.dot(hid, wd_ref[0, 0].astype(BF16), preferred_element_type=F32)
        half = y.shape[1] // 2
        y_cur[...] = _pack_bf16_pair(y[:, :half], y[:, half:])

    @pl.when(i % 2 == 0)
    def _():
        issue(xbuf1, ybuf1, gsem.at[1])

    @pl.when(i % 2 == 0)
    def _():
        compute(xbuf0, ybuf0, gsem.at[0])

    @pl.when(i % 2 == 1)
    def _():
        issue(xbuf0, ybuf0, gsem.at[0])

    @pl.when(i % 2 == 1)
    def _():
        compute(xbuf1, ybuf1, gsem.at[1])

    last = (nblk - 1) % 2
    x_extra, g_extra, y_last = (xbuf1, gsem.at[1], ybuf0) if last == 0 else (xbuf0, gsem.at[0], ybuf1)

    @pl.when(i == nblk - 1)
    def _():
        gather_done(x_extra, g_extra)
        scatter_done()

        def body(r, carry):
            _row_copy(y_last, r, y_hbm, dstc_ref[0, 0, r], ssem.at[0]).start()
            return carry
        lax.fori_loop(0, tm, body, 0)
        scatter_done()


def _experts(h2, slot_tok, slot_dst, blk_expert, w_gate, w_up, w_down, layer, tm, y_rows):
    d, de = w_gate.shape[2], w_gate.shape[3]
    dp = h2.shape[1]
    nblk = blk_expert.shape[0]
    tok3 = slot_tok.reshape(nblk, 1, tm)
    dst3 = slot_dst.reshape(nblk + 1, 1, tm)
    smem_blk = lambda f: pl.BlockSpec((1, 1, tm), f, memory_space=pltpu.SMEM)
    grid_spec = pltpu.PrefetchScalarGridSpec(
        num_scalar_prefetch=1,
        grid=(nblk,),
        in_specs=[smem_blk(lambda i, be: (i, 0, 0)),
                  smem_blk(lambda i, be: (jnp.minimum(i + 1, nblk - 1), 0, 0)),
                  smem_blk(lambda i, be: (i, 0, 0)),
                  smem_blk(lambda i, be: (i + 1, 0, 0)),
                  pl.BlockSpec(memory_space=pl.ANY),
                  pl.BlockSpec((1, 1, d, de), lambda i, be: (layer, be[i], 0, 0)),
                  pl.BlockSpec((1, 1, d, de), lambda i, be: (layer, be[i], 0, 0)),
                  pl.BlockSpec((1, 1, de, d), lambda i, be: (layer, be[i], 0, 0))],
        out_specs=pl.BlockSpec(memory_space=pl.ANY),
        scratch_shapes=[pltpu.VMEM((tm, dp), jnp.uint32)] * 4
        + [pltpu.SemaphoreType.DMA((2,)), pltpu.SemaphoreType.DMA((1,))],
    )
    return pl.pallas_call(
        functools.partial(_experts_kernel, tm=tm),
        grid_spec=grid_spec,
        out_shape=jax.ShapeDtypeStruct((y_rows, dp), jnp.uint32),
        compiler_params=_cparams("arbitrary"),
    )(blk_expert, tok3, tok3, dst3, dst3, h2, w_gate, w_up, w_down)


def _combine_kernel(*refs, bt, rows, alpha):
    y_refs = refs[:TOPK]
    x1_ref, h_ref, gw_ref, g2_ref, wg_ref, wu_ref, wd_ref, lg_ref, lb_ref, o_ref = refs[TOPK:]
    x = _unpack_rows_bf16(h_ref[...])
    g = jnp.dot(x, wg_ref[...], preferred_element_type=F32)
    u = jnp.dot(x, wu_ref[...], preferred_element_type=F32)
    hid = (g * _sigmoid(g) * u).astype(BF16)
    moe = jnp.dot(hid, wd_ref[...], preferred_element_type=F32)
    half = moe.shape[1] // 2
    moe_lo, moe_hi = moe[:, :half], moe[:, half:]
    for j in range(TOPK):
        y_lo, y_hi = _unpack_bf16_pair(y_refs[j][...])
        w = gw_ref[:, j:j + 1]
        moe_lo = moe_lo + y_lo * w
        moe_hi = moe_hi + y_hi * w
    moe = jnp.concatenate([moe_lo, moe_hi], axis=1)
    y = alpha * x1_ref[...] + _per_batch(lambda m, g2: (1.0 + g2) * m, moe, (g2_ref,), bt, rows)
    o_ref[...] = _layer_norm(y, lg_ref[...], lb_ref[...])


def _combine(y_rows, gw, x1, h2, gate2, w_gate, w_up, w_down, ln_g, ln_b, rows_per_batch, alpha):
    t, d = x1.shape
    dp = h2.shape[1]
    tt, bt, vec_idx = _batch_tiling(t, rows_per_batch, 128)
    nt = t // tt
    row = pl.BlockSpec((1, d), lambda i: (0, 0))
    tile = pl.BlockSpec((tt, d), lambda i: (i, 0))
    packed_tile = pl.BlockSpec((tt, dp), lambda i: (i, 0))
    full = lambda a: pl.BlockSpec(a.shape, lambda i: (0, 0))
    slab = lambda j: pl.BlockSpec((tt, dp), lambda i: (j * nt + i, 0))
    return pl.pallas_call(
        functools.partial(_combine_kernel, bt=bt, rows=rows_per_batch, alpha=alpha),
        grid=(nt,),
        in_specs=[slab(j) for j in range(TOPK)] + [
            tile, packed_tile, pl.BlockSpec((tt, TOPK), lambda i: (i, 0)),
            pl.BlockSpec((bt, 1, d), lambda i: (vec_idx(i), 0, 0)),
            full(w_gate), full(w_up), full(w_down), row, row],
        out_specs=tile,
        out_shape=jax.ShapeDtypeStruct((t, d), F32),
        compiler_params=_cparams("arbitrary"),
    )(*([y_rows] * TOPK), x1, h2, gw, gate2, w_gate, w_up, w_down, ln_g, ln_b)


def _dispatch_plan(eidx_t, rank_t, counts, tm):
    k, t = eidx_t.shape
    n_experts = counts.shape[0]
    nblk = (t * k + tm - 1) // tm + n_experts
    padded = (counts + tm - 1) // tm * tm
    pend = jnp.cumsum(padded)
    pstart = pend - padded
    experts = jnp.arange(n_experts, dtype=I32)
    start_of = jnp.sum(jnp.where(eidx_t[:, :, None] == experts, pstart, 0), axis=-1)
    pos = (start_of + rank_t).reshape(-1)
    flat = jnp.full((nblk * tm,), -1, I32).at[pos].set(jnp.arange(k * t, dtype=I32), unique_indices=True)
    real = flat >= 0
    slot_tok = jnp.where(real, flat % t, 0)
    dump = k * t + jnp.arange(nblk * tm, dtype=I32) % tm
    slot_dst = jnp.concatenate([dump[:tm], jnp.where(real, flat, dump)])
    blk_row = jnp.arange(nblk, dtype=I32) * tm
    blk_expert = jnp.minimum(jnp.sum((pend[None, :] <= blk_row[:, None]).astype(I32), axis=1), n_experts - 1)
    return slot_tok, slot_dst, blk_expert


def _layer(x, mods, layer, wts, rows_per_batch, alpha, mix_state):
    b, s, d = x.shape
    t = b * s
    sh1, sc1, g1, sh2, sc2, g2 = mods
    x2 = x.reshape(t, d)
    n_heads = wts['b_f'].shape[1]
    da = n_heads * HEAD_DIM
    dc = d - da
    assert da == dc, "column-block indexing of the projection assumes equal attention / conv widths"
    cb = wts['conv_w'].shape[1] - 1

    p, flog = _in_proj(x2, sh1, sc1, wts['w_in_main'][layer], wts['w_in_f'][layer], rows_per_batch)
    p3 = p.reshape(b, s, p.shape[1])
    k_new = p3[:, :, da:2 * da]
    v_new = p3[:, :, 2 * da:3 * da]
    b_f_row = wts['b_f'][layer][None, :]
    conv_args = (wts['conv_w'][layer], wts['conv_b'][layer][None, :], wts['conv_ln_g'][layer][None, :],
                 wts['conv_ln_b'][layer][None, :])

    if mix_state is None:
        logf, _, aug_q, aug_k = _forget_cumsum(flog.reshape(b, s, n_heads), b_f_row,
                                               jnp.zeros((b, 1, n_heads), F32), True)
        attn = _attention(p3, 0, p3, 1, p3, 2, aug_q, aug_k, 0)
        conv, tail = _conv_module(p3, 3, p3, 3, 4, False, *conv_args)
    else:
        ck_all, cv_all, clf, cconv = mix_state
        past = ck_all.shape[1]
        _, cum_c, _, aug_kc = _forget_cumsum(clf, b_f_row, jnp.zeros((b, 1, n_heads), F32), False)
        logf, _, aug_q, aug_kn = _forget_cumsum(flog.reshape(b, s, n_heads), b_f_row,
                                                cum_c[:, past - 1:past, :], True)
        attn = _attention(p3, 0, ck_all, 0, cv_all, 0, aug_q, aug_kc, past, kv_batch_off=layer * b,
                          tail=(p3, 1, p3, 2, aug_kn))
        hist = jnp.pad(cconv, ((0, 0), (CONV_HALO - cb, 0), (0, 0)))
        conv, tail = _conv_module(p3, 3, hist, 0, 0, True, *conv_args)

    x1, h, logits = _out_proj(attn.reshape(t, da), conv.reshape(t, dc), x2, g1, sh2, sc2,
                              wts['w_out_a'][layer], wts['w_out_c'][layer],
                              wts['ln1_g'][layer][None, :], wts['ln1_b'][layer][None, :],
                              wts['wr_hi'][layer], wts['wr_lo'][layer], rows_per_batch, alpha)

    eidx_t, gw_t, rank_t, counts = _route(logits, wts['e_bias'][layer][:, None])
    tm_e = _tile(t, 256)
    slot_tok, slot_dst, blk_expert = _dispatch_plan(eidx_t, rank_t, counts[:, 0].astype(I32), tm_e)
    y_rows = _experts(h, slot_tok, slot_dst, blk_expert, wts['w_e_gate'], wts['w_e_up'], wts['w_e_down'],
                      layer, tm_e, TOPK * t + tm_e)
    x_out = _combine(y_rows, gw_t.T, x1, h, g2, wts['w_s_gate'][layer], wts['w_s_up'][layer],
                     wts['w_s_down'][layer], wts['ln2_g'][layer][None, :], wts['ln2_b'][layer][None, :],
                     rows_per_batch, alpha)

    state = (k_new.reshape(b, s, n_heads, HEAD_DIM), v_new.reshape(b, s, n_heads, HEAD_DIM), logf,
             tail[:, CONV_HALO - cb:, :])
    return x_out.reshape(b, s, d), state


def kernel(x_prompt, x_sample, c_prompt, c_sample, cache_k, cache_v, cache_logf, state_conv, w_ada, b_ada, w_in, b_f, conv_w, conv_b, conv_ln_g, conv_ln_b, w_out, ln1_g, ln1_b, w_router, e_bias, w_e_gate, w_e_up, w_e_down, w_s_gate, w_s_up, w_s_down, ln2_g, ln2_b):
    depth, d, _ = w_in.shape
    n_heads = b_f.shape[1]
    da = n_heads * HEAD_DIM
    n_experts = w_router.shape[2]
    alpha = (2 * depth) ** 0.25
    bp = x_prompt.shape[0]

    fo = 3 * da
    wr_pad = jnp.pad(w_router, ((0, 0), (0, 0), (0, ROUTER_LANES - n_experts)))
    wr_hi = wr_pad.astype(BF16)
    wts = dict(
        w_in_main=jnp.concatenate([w_in[:, :, :fo], w_in[:, :, fo + n_heads:]], axis=2).astype(BF16),
        w_in_f=w_in[:, :, fo:fo + n_heads].astype(BF16),
        b_f=b_f, conv_w=conv_w, conv_b=conv_b, conv_ln_g=conv_ln_g, conv_ln_b=conv_ln_b,
        w_out_a=w_out[:, :da, :].astype(BF16), w_out_c=w_out[:, da:, :].astype(BF16),
        ln1_g=ln1_g, ln1_b=ln1_b, ln2_g=ln2_g, ln2_b=ln2_b,
        wr_hi=wr_hi, wr_lo=(wr_pad - wr_hi.astype(F32)).astype(BF16), e_bias=e_bias,
        w_e_gate=w_e_gate, w_e_up=w_e_up, w_e_down=w_e_down,
        w_s_gate=w_s_gate.astype(BF16), w_s_up=w_s_up.astype(BF16), w_s_down=w_s_down.astype(BF16),
    )

    mod_all = _adaln(jnp.concatenate([c_prompt, c_sample], axis=0), w_ada, b_ada)

    def mods(layer, lo, hi):
        m = mod_all[layer, lo:hi]
        return tuple(m[:, None, i * d:(i + 1) * d] for i in range(6))

    xp, xs = x_prompt, x_sample
    outs_p, outs_s = [], []
    for layer in range(depth):
        xp, st = _layer(xp, mods(layer, 0, bp), layer, wts, x_prompt.shape[1], alpha, None)
        outs_p.append(st)
        cache = (cache_k.reshape(-1, cache_k.shape[2], da), cache_v.reshape(-1, cache_v.shape[2], da),
                 cache_logf[layer], state_conv[layer])
        xs, st = _layer(xs, mods(layer, bp, bp + x_sample.shape[0]), layer, wts, x_sample.shape[1], alpha, cache)
        outs_s.append(st)

    stack = lambda outs, i: jnp.stack([o[i] for o in outs])
    return (xp, xs, stack(outs_p, 0), stack(outs_p, 1), stack(outs_p, 2), stack(outs_p, 3),
            stack(outs_s, 0), stack(outs_s, 1), stack(outs_s, 2), stack(outs_s, 3))
```

```python
import functools

import jax
import jax.numpy as jnp
from jax import lax
from jax.experimental import pallas as pl
from jax.experimental.pallas import tpu as pltpu

F32 = jnp.float32
BF16 = jnp.bfloat16
I32 = jnp.int32

HEAD_DIM = 128
N_GROUPS = 8
TOPK_GROUPS = 4
TOPK = 8
ROUTED_SCALE = 2.5
LN_EPS = 1e-5
NEG_BIG = -1e30

CONV_HALO = 32
ROUTER_LANES = 128
VMEM_LIMIT = 56 * 1024 * 1024


def _cparams(*sem):
    return pltpu.CompilerParams(dimension_semantics=sem, vmem_limit_bytes=VMEM_LIMIT)


def _tile(n, target, mult=8):
    if n <= target:
        return n
    t = (target // mult) * mult
    while t >= mult:
        if n % t == 0:
            return t
        t -= mult
    return n


def _sigmoid(x):
    return 1.0 / (1.0 + jnp.exp(-x))


def _log_sigmoid(x):
    return jnp.minimum(x, 0.0) - jnp.log(1.0 + jnp.exp(-jnp.abs(x)))


def _layer_norm(y, g, b):
    mu = jnp.mean(y, axis=-1, keepdims=True)
    d = y - mu
    var = jnp.mean(d * d, axis=-1, keepdims=True)
    return d * lax.rsqrt(var + LN_EPS) * g + b


def _pack_bf16_pair(lo, hi):
    lo_bits = lax.bitcast_convert_type(lo.astype(BF16).astype(F32), jnp.uint32)
    hi_bits = lax.bitcast_convert_type(hi.astype(BF16).astype(F32), jnp.uint32)
    return (lo_bits >> 16) | (hi_bits & jnp.uint32(0xFFFF0000))


def _unpack_bf16_pair(w):
    lo = lax.bitcast_convert_type(w << 16, F32)
    hi = lax.bitcast_convert_type(w & jnp.uint32(0xFFFF0000), F32)
    return lo, hi


def _unpack_rows_bf16(w):
    lo, hi = _unpack_bf16_pair(w)
    return jnp.concatenate([lo.astype(BF16), hi.astype(BF16)], axis=1)


def _per_batch(fn, x, refs, bt, rows):
    if bt == 1:
        return fn(x, *[r[0] for r in refs])
    tm, d = x.shape
    out = fn(x.reshape(bt, rows, d), *[r[...] for r in refs])
    return out.reshape(tm, d)


def _batch_tiling(n_rows, rows_per_batch, target):
    if rows_per_batch >= target:
        tm = _tile(rows_per_batch, target)
        per = rows_per_batch // tm
        return tm, 1, (lambda i: i // per)
    nb = n_rows // rows_per_batch
    bt = _tile(nb, max(1, target // rows_per_batch), 1)
    return bt * rows_per_batch, bt, (lambda i: i)


def _adaln_kernel(c_ref, w_ref, b_ref, o_ref):
    c = c_ref[...]
    a = (c * _sigmoid(c)).astype(BF16)
    o_ref[0] = jnp.dot(a, w_ref[0].astype(BF16), preferred_element_type=F32) + b_ref[0]


def _adaln(c_all, w_ada, b_ada):
    depth, d, n6 = w_ada.shape
    bc = c_all.shape[0]
    tn = _tile(n6, 1024, 128)
    return pl.pallas_call(
        _adaln_kernel,
        grid=(depth, n6 // tn),
        in_specs=[pl.BlockSpec((bc, d), lambda l, j: (0, 0)),
                  pl.BlockSpec((1, d, tn), lambda l, j: (l, 0, j)),
                  pl.BlockSpec((1, 1, tn), lambda l, j: (l, 0, j))],
        out_specs=pl.BlockSpec((1, bc, tn), lambda l, j: (l, 0, j)),
        out_shape=jax.ShapeDtypeStruct((depth, bc, n6), F32),
        compiler_params=_cparams("arbitrary", "arbitrary"),
    )(c_all, w_ada, b_ada.reshape(depth, 1, n6))


def _in_proj_kernel(x_ref, sh_ref, sc_ref, w_ref, wf_ref, p_ref, f_ref, u_ref, *, bt, rows):
    @pl.when(pl.program_id(1) == 0)
    def _():
        u = _per_batch(lambda x, sh, sc: x * (1.0 + sc) + sh, x_ref[...], (sh_ref, sc_ref), bt, rows)
        ub = u.astype(BF16)
        u_ref[...] = ub
        f_ref[...] = jnp.dot(ub, wf_ref[...], preferred_element_type=F32)

    p_ref[...] = jnp.dot(u_ref[...], w_ref[...], preferred_element_type=F32)


def _in_proj(x2, shift, scale, w_main, w_f, rows_per_batch):
    t, d = x2.shape
    n_main = w_main.shape[1]
    h = w_f.shape[1]
    tm, bt, vec_idx = _batch_tiling(t, rows_per_batch, 1024)
    tn = _tile(n_main, 512, 128)
    vec_spec = pl.BlockSpec((bt, 1, d), lambda i, j: (vec_idx(i), 0, 0))
    return pl.pallas_call(
        functools.partial(_in_proj_kernel, bt=bt, rows=rows_per_batch),
        grid=(t // tm, n_main // tn),
        in_specs=[pl.BlockSpec((tm, d), lambda i, j: (i, 0)), vec_spec, vec_spec,
                  pl.BlockSpec((d, tn), lambda i, j: (0, j)),
                  pl.BlockSpec((d, h), lambda i, j: (0, 0))],
        out_specs=[pl.BlockSpec((tm, tn), lambda i, j: (i, j)),
                   pl.BlockSpec((tm, h), lambda i, j: (i, 0))],
        out_shape=[jax.ShapeDtypeStruct((t, n_main), F32), jax.ShapeDtypeStruct((t, h), F32)],
        scratch_shapes=[pltpu.VMEM((tm, d), BF16)],
        compiler_params=_cparams("arbitrary", "arbitrary"),
    )(x2, shift, scale, w_main, w_f)


LOG2E = 1.4426950408889634
AUG_TERMS = 3


def _bias_placement(n_heads):
    rows = (AUG_TERMS + 1) * n_heads
    eq = [[0.0] * (n_heads * HEAD_DIM) for _ in range(rows)]
    ek = [[0.0] * (n_heads * HEAD_DIM) for _ in range(rows)]
    for h in range(n_heads):
        for n in range(AUG_TERMS):
            eq[n * n_heads + h][h * HEAD_DIM + n] = 1.0
            ek[n * n_heads + h][h * HEAD_DIM + AUG_TERMS + n] = -1.0
            eq[AUG_TERMS * n_heads + h][h * HEAD_DIM + AUG_TERMS + n] = 1.0
            ek[AUG_TERMS * n_heads + h][h * HEAD_DIM + n] = 1.0
    return jnp.array(eq, BF16), jnp.array(ek, BF16)


def _cumsum_kernel(z_ref, bf_ref, c0_ref, eq_ref, ek_ref, lf_ref, cum_ref, aq_ref, ak_ref, *, logsig, chunk):
    s, n_heads = z_ref.shape[1], z_ref.shape[2]
    r = lax.broadcasted_iota(I32, (chunk, chunk), 0)
    c = lax.broadcasted_iota(I32, (chunk, chunk), 1)
    tri = (r >= c).astype(F32)
    carry = c0_ref[0]
    for ci in range(s // chunk):
        sl = slice(ci * chunk, (ci + 1) * chunk)
        z = z_ref[0, sl, :]
        lf = _log_sigmoid(z + bf_ref[...]) if logsig else z
        lf_ref[0, sl, :] = lf
        cum = jnp.dot(tri, lf, precision=lax.Precision.HIGHEST, preferred_element_type=F32) + carry
        cum_ref[0, sl, :] = cum
        carry = cum[chunk - 1:chunk, :]
        rem = cum * LOG2E
        pieces = []
        for _ in range(AUG_TERMS):
            piece = rem.astype(BF16).astype(F32)
            pieces.append(piece)
            rem = rem - piece
        pm = jnp.concatenate(pieces + [jnp.ones((chunk, n_heads), F32)], axis=1).astype(BF16)
        aq_ref[0, sl, :] = jnp.dot(pm, eq_ref[...], preferred_element_type=F32).astype(BF16)
        ak_ref[0, sl, :] = jnp.dot(pm, ek_ref[...], preferred_element_type=F32).astype(BF16)


def _forget_cumsum(z, b_f_row, carry0, logsig):
    b, s, h = z.shape
    chunk = _tile(s, 256)
    blk = pl.BlockSpec((1, s, h), lambda i: (i, 0, 0))
    aug = pl.BlockSpec((1, s, h * HEAD_DIM), lambda i: (i, 0, 0))
    eq, ek = _bias_placement(h)
    place = pl.BlockSpec(eq.shape, lambda i: (0, 0))
    return pl.pallas_call(
        functools.partial(_cumsum_kernel, logsig=logsig, chunk=chunk),
        grid=(b,),
        in_specs=[blk, pl.BlockSpec((1, h), lambda i: (0, 0)), pl.BlockSpec((1, 1, h), lambda i: (i, 0, 0)),
                  place, place],
        out_specs=[blk, blk, aug, aug],
        out_shape=[jax.ShapeDtypeStruct((b, s, h), F32)] * 2
        + [jax.ShapeDtypeStruct((b, s, h * HEAD_DIM), BF16)] * 2,
        compiler_params=_cparams("arbitrary"),
    )(z, b_f_row, carry0, eq, ek)


def _attn_kernel(*refs, n_heads, tq, tk, q_off, scale, tail_rows):
    if tail_rows:
        q_ref, k_ref, v_ref, aq_ref, ak_ref, kt_ref, vt_ref, akt_ref, o_ref, m_ref, acc_ref = refs
    else:
        q_ref, k_ref, v_ref, aq_ref, ak_ref, o_ref, m_ref, acc_ref = refs
    qi = pl.program_id(1)
    ki = pl.program_id(2)
    n_main = pl.num_programs(2) - (1 if tail_rows else 0)

    @pl.when(ki == 0)
    def _():
        m_ref[...] = jnp.full(m_ref.shape, NEG_BIG, F32)
        acc_ref[...] = jnp.zeros(acc_ref.shape, F32)

    q_start = qi * tq + q_off

    def attend(kr, vr, akr, k_start, width):
        qpos = q_start + lax.broadcasted_iota(I32, (tq, width), 0)
        kpos = k_start + lax.broadcasted_iota(I32, (tq, width), 1)
        visible = kpos <= qpos
        ones = jnp.ones((width, HEAD_DIM), BF16)
        for h in range(n_heads):
            hs = slice(h * HEAD_DIM, (h + 1) * HEAD_DIM)
            qa = jnp.concatenate([(q_ref[0, :, hs] * (scale * LOG2E)).astype(BF16), aq_ref[0, :, hs]], axis=1)
            ka = jnp.concatenate([kr[0, :, hs].astype(BF16), akr[0, :, hs]], axis=1)
            va = jnp.concatenate([vr[0, :, hs].astype(BF16), ones], axis=1)
            s = lax.dot_general(qa, ka, (((1,), (1,)), ((), ())), preferred_element_type=F32)
            s = jnp.where(visible, s, NEG_BIG)
            m_prev = m_ref[h]
            m_new = jnp.maximum(m_prev, jnp.max(s, axis=-1, keepdims=True))
            p = jnp.exp2(s - m_new).astype(BF16)
            acc_ref[h] = jnp.exp2(m_prev - m_new) * acc_ref[h] + jnp.dot(p, va, preferred_element_type=F32)
            m_ref[h] = m_new

    @pl.when((ki < n_main) & (ki * tk <= q_start + (tq - 1)))
    def _():
        attend(k_ref, v_ref, ak_ref, ki * tk, tk)

    if tail_rows:
        @pl.when(ki == n_main)
        def _():
            attend(kt_ref, vt_ref, akt_ref, n_main * tk, tail_rows)

    @pl.when(ki == pl.num_programs(2) - 1)
    def _():
        for h in range(n_heads):
            acc = acc_ref[h]
            o_ref[0, :, h * HEAD_DIM:(h + 1) * HEAD_DIM] = (
                acc[:, :HEAD_DIM] / acc[:, HEAD_DIM:HEAD_DIM + 1]).astype(o_ref.dtype)


def _attention(q_arr, q_col, k_arr, k_col, v_arr, v_col, aug_q, aug_k, q_off, kv_batch_off=0, tail=None):
    b, sq, da = aug_q.shape
    sk = aug_k.shape[1]
    n_heads = da // HEAD_DIM
    tq = _tile(sq, 512)
    tk = _tile(sk, 512)
    n_main = sk // tk

    def k_blk(i, j):
        return jnp.minimum(jnp.minimum(j, (i * tq + q_off + tq - 1) // tk), n_main - 1)

    in_specs = [pl.BlockSpec((1, tq, da), lambda bi, i, j: (bi, i, q_col)),
                pl.BlockSpec((1, tk, da), lambda bi, i, j: (kv_batch_off + bi, k_blk(i, j), k_col)),
                pl.BlockSpec((1, tk, da), lambda bi, i, j: (kv_batch_off + bi, k_blk(i, j), v_col)),
                pl.BlockSpec((1, tq, da), lambda bi, i, j: (bi, i, 0)),
                pl.BlockSpec((1, tk, da), lambda bi, i, j: (bi, k_blk(i, j), 0))]
    args = [q_arr, k_arr, v_arr, aug_q, aug_k]
    tail_rows = 0
    if tail is not None:
        tk_arr, tk_col, tv_arr, tv_col, t_aug = tail
        tail_rows = t_aug.shape[1]
        in_specs += [pl.BlockSpec((1, tail_rows, da), lambda bi, i, j: (bi, 0, tk_col)),
                     pl.BlockSpec((1, tail_rows, da), lambda bi, i, j: (bi, 0, tv_col)),
                     pl.BlockSpec((1, tail_rows, da), lambda bi, i, j: (bi, 0, 0))]
        args += [tk_arr, tv_arr, t_aug]

    kern = functools.partial(_attn_kernel, n_heads=n_heads, tq=tq, tk=tk, q_off=q_off, scale=HEAD_DIM ** -0.5,
                             tail_rows=tail_rows)
    return pl.pallas_call(
        kern,
        grid=(b, sq // tq, n_main + (1 if tail_rows else 0)),
        in_specs=in_specs,
        out_specs=pl.BlockSpec((1, tq, da), lambda bi, i, j: (bi, i, 0)),
        out_shape=jax.ShapeDtypeStruct((b, sq, da), BF16),
        scratch_shapes=[pltpu.VMEM((n_heads, tq, 1), F32), pltpu.VMEM((n_heads, tq, 2 * HEAD_DIM), F32)],
        compiler_params=_cparams("arbitrary", "arbitrary", "arbitrary"),
    )(*args)


SUBLANES = 8


def _conv_kernel(a_ref, b_ref, ha_ref, hb_ref, w_ref, cb_ref, g_ref, be_ref, o_ref, tail_ref, xbuf, xsh, ybuf,
                 *, tt, kw, halo_is_glu, zero_first):
    glu = a_ref[0] * _sigmoid(b_ref[0])
    xbuf[CONV_HALO:CONV_HALO + tt, :] = glu
    halo = ha_ref[0] if halo_is_glu else ha_ref[0] * _sigmoid(hb_ref[0])
    if zero_first:
        halo = jnp.where(pl.program_id(1) == 0, 0.0, halo)
    xbuf[0:CONV_HALO, :] = halo
    tail_ref[0] = xbuf[tt:tt + CONV_HALO, :]

    span = tt + CONV_HALO - SUBLANES
    for s in range(1, SUBLANES):
        xsh[s - 1, 0:span, :] = xbuf[s:s + span, :]

    channels = xbuf.shape[1]
    rc = _tile(tt, 64)
    first = CONV_HALO - (kw - 1)
    for c0 in range(0, channels, 128):
        cs = slice(c0, c0 + 128)
        for r0 in range(0, tt, rc):
            acc = jnp.zeros((rc, 128), F32)
            for j in range(kw):
                shift = (first + j) % SUBLANES
                base = (first + j) - shift + r0
                src = xbuf if shift == 0 else xsh.at[shift - 1]
                acc = acc + src[base:base + rc, cs] * w_ref[j:j + 1, cs]
            ybuf[r0:r0 + rc, cs] = acc
    y = _layer_norm(ybuf[...] + cb_ref[...], g_ref[...], be_ref[...])
    o_ref[0] = (y * _sigmoid(y)).astype(o_ref.dtype)


def _conv_module(p3, a_col, halo_arr, halo_a_col, halo_b_col, halo_is_glu, conv_w, conv_b, ln_g, ln_b):
    b, s, _ = p3.shape
    kw, c = conv_w.shape
    assert kw - 1 <= CONV_HALO
    tt = _tile(s, 128)
    per = tt // CONV_HALO if tt >= CONV_HALO else 1

    def halo_row(i):
        return jnp.maximum(i * per - 1, 0)

    kern = functools.partial(_conv_kernel, tt=tt, kw=kw, halo_is_glu=halo_is_glu, zero_first=not halo_is_glu)
    vec = pl.BlockSpec((1, c), lambda bi, i: (0, 0))
    return pl.pallas_call(
        kern,
        grid=(b, s // tt),
        in_specs=[pl.BlockSpec((1, tt, c), lambda bi, i: (bi, i, a_col)),
                  pl.BlockSpec((1, tt, c), lambda bi, i: (bi, i, a_col + 1)),
                  pl.BlockSpec((1, CONV_HALO, c), lambda bi, i: (bi, halo_row(i), halo_a_col)),
                  pl.BlockSpec((1, CONV_HALO, c), lambda bi, i: (bi, halo_row(i), halo_b_col)),
                  pl.BlockSpec((kw, c), lambda bi, i: (0, 0)), vec, vec, vec],
        out_specs=[pl.BlockSpec((1, tt, c), lambda bi, i: (bi, i, 0)),
                   pl.BlockSpec((1, CONV_HALO, c), lambda bi, i: (bi, 0, 0))],
        out_shape=[jax.ShapeDtypeStruct((b, s, c), BF16), jax.ShapeDtypeStruct((b, CONV_HALO, c), F32)],
        scratch_shapes=[pltpu.VMEM((CONV_HALO + tt, c), F32), pltpu.VMEM((SUBLANES - 1, CONV_HALO + tt, c), F32),
                        pltpu.VMEM((tt, c), F32)],
        compiler_params=_cparams("arbitrary", "arbitrary"),
    )(p3, p3, halo_arr, halo_arr, conv_w, conv_b, ln_g, ln_b)


def _out_proj_kernel(at_ref, cv_ref, x_ref, g1_ref, sh2_ref, sc2_ref, wa_ref, wc_ref, lg_ref, lb_ref,
                     wrh_ref, wrl_ref, x1_ref, h_ref, lo_ref, *, bt, rows, alpha):
    mix = (jnp.dot(at_ref[...], wa_ref[...], preferred_element_type=F32)
           + jnp.dot(cv_ref[...], wc_ref[...], preferred_element_type=F32))
    y = alpha * x_ref[...] + _per_batch(lambda m, g: (1.0 + g) * m, mix, (g1_ref,), bt, rows)
    x1 = _layer_norm(y, lg_ref[...], lb_ref[...])
    x1_ref[...] = x1
    h = _per_batch(lambda x, sh, sc: x * (1.0 + sc) + sh, x1, (sh2_ref, sc2_ref), bt, rows)
    half = h.shape[1] // 2
    h_ref[...] = _pack_bf16_pair(h[:, :half], h[:, half:])
    hh = h.astype(BF16)
    hl = (h - hh.astype(F32)).astype(BF16)
    lo_ref[...] = (jnp.dot(hh, wrh_ref[...], preferred_element_type=F32)
                   + jnp.dot(hh, wrl_ref[...], preferred_element_type=F32)
                   + jnp.dot(hl, wrh_ref[...], preferred_element_type=F32))


def _out_proj(attn2, conv2, x2, gate1, shift2, scale2, w_out_a, w_out_c, ln_g, ln_b, wr_hi, wr_lo,
              rows_per_batch, alpha):
    t, d = x2.shape
    da = attn2.shape[1]
    dc = conv2.shape[1]
    tm, bt, vec_idx = _batch_tiling(t, rows_per_batch, 256)
    vec_spec = pl.BlockSpec((bt, 1, d), lambda i: (vec_idx(i), 0, 0))
    row = pl.BlockSpec((1, d), lambda i: (0, 0))
    full = lambda a: pl.BlockSpec(a.shape, lambda i: (0, 0))
    return pl.pallas_call(
        functools.partial(_out_proj_kernel, bt=bt, rows=rows_per_batch, alpha=alpha),
        grid=(t // tm,),
        in_specs=[pl.BlockSpec((tm, da), lambda i: (i, 0)), pl.BlockSpec((tm, dc), lambda i: (i, 0)),
                  pl.BlockSpec((tm, d), lambda i: (i, 0)), vec_spec, vec_spec, vec_spec,
                  full(w_out_a), full(w_out_c), row, row, full(wr_hi), full(wr_lo)],
        out_specs=[pl.BlockSpec((tm, d), lambda i: (i, 0)), pl.BlockSpec((tm, d // 2), lambda i: (i, 0)),
                   pl.BlockSpec((tm, ROUTER_LANES), lambda i: (i, 0))],
        out_shape=[jax.ShapeDtypeStruct((t, d), F32), jax.ShapeDtypeStruct((t, d // 2), jnp.uint32),
                   jax.ShapeDtypeStruct((t, ROUTER_LANES), F32)],
        compiler_params=_cparams("arbitrary"),
    )(attn2, conv2, x2, gate1, shift2, scale2, w_out_a, w_out_c, ln_g, ln_b, wr_hi, wr_lo)


def _route_kernel(lo_ref, eb_ref, eidx_ref, gw_ref, rank_ref, cnt_ref, carry_ref, *, n_experts, tr):
    i = pl.program_id(0)
    gsz = n_experts // N_GROUPS

    @pl.when(i == 0)
    def _():
        carry_ref[...] = jnp.zeros(carry_ref.shape, F32)

    logits = lo_ref[...].T[0:n_experts, :]
    sub = lax.broadcasted_iota(I32, (gsz, tr), 0)
    s_g, sb_g, gid_g = [], [], []
    for g in range(N_GROUPS):
        rs = slice(g * gsz, (g + 1) * gsz)
        s = _sigmoid(logits[rs, :])
        s_g.append(s)
        sb_g.append(s + eb_ref[rs, :])
        gid_g.append(sub + g * gsz)

    gscore = []
    for g in range(N_GROUPS):
        x = sb_g[g]
        m1 = jnp.max(x, axis=0, keepdims=True)
        i1 = jnp.min(jnp.where(x == m1, sub, gsz), axis=0, keepdims=True)
        m2 = jnp.max(jnp.where(sub == i1, -jnp.inf, x), axis=0, keepdims=True)
        gscore.append(m1 + m2)
    gs = jnp.concatenate(gscore, axis=0)
    gi = lax.broadcasted_iota(I32, (N_GROUPS, tr), 0)
    sel = jnp.zeros((N_GROUPS, tr), F32)
    for _ in range(TOPK_GROUPS):
        gm = jnp.max(gs, axis=0, keepdims=True)
        pick = gi == jnp.min(jnp.where(gs == gm, gi, N_GROUPS), axis=0, keepdims=True)
        sel = jnp.where(pick, 1.0, sel)
        gs = jnp.where(pick, -jnp.inf, gs)

    cand = [jnp.where(sel[g:g + 1, :] > 0.0, sb_g[g], -jnp.inf) for g in range(N_GROUPS)]
    member = [jnp.zeros((gsz, tr), F32) for _ in range(N_GROUPS)]
    eidx, wsel = [], []
    for _ in range(TOPK):
        mx = cand[0]
        for g in range(1, N_GROUPS):
            mx = jnp.maximum(mx, cand[g])
        mx = jnp.max(mx, axis=0, keepdims=True)
        ei = jnp.where(cand[0] == mx, gid_g[0], n_experts)
        for g in range(1, N_GROUPS):
            ei = jnp.minimum(ei, jnp.where(cand[g] == mx, gid_g[g], n_experts))
        ei = jnp.min(ei, axis=0, keepdims=True)
        w = jnp.zeros((gsz, tr), F32)
        for g in range(N_GROUPS):
            hit = gid_g[g] == ei
            w = w + jnp.where(hit, s_g[g], 0.0)
            member[g] = jnp.where(hit, 1.0, member[g])
            cand[g] = jnp.where(hit, -jnp.inf, cand[g])
        eidx.append(ei)
        wsel.append(jnp.sum(w, axis=0, keepdims=True))
    w8 = jnp.concatenate(wsel, axis=0)
    gw_ref[...] = w8 / jnp.sum(w8, axis=0, keepdims=True) * ROUTED_SCALE
    eidx_ref[...] = jnp.concatenate(eidx, axis=0)

    mt = jnp.concatenate(member, axis=0)
    r = lax.broadcasted_iota(I32, (tr, tr), 0)
    c = lax.broadcasted_iota(I32, (tr, tr), 1)
    before = (r < c).astype(BF16)
    prior = jnp.dot(mt.astype(BF16), before, preferred_element_type=F32) + carry_ref[...]
    eio = lax.broadcasted_iota(I32, (n_experts, tr), 0)
    ranks = [jnp.sum(jnp.where(eio == ei, prior, 0.0), axis=0, keepdims=True) for ei in eidx]
    rank_ref[...] = jnp.concatenate(ranks, axis=0).astype(I32)
    carry_ref[...] = carry_ref[...] + jnp.sum(mt, axis=1, keepdims=True)
    cnt_ref[...] = carry_ref[...]


def _route(logits, e_bias_col):
    t = logits.shape[0]
    n_experts = e_bias_col.shape[0]
    tr = _tile(t, 512, 128)
    out_blk = pl.BlockSpec((TOPK, tr), lambda i: (0, i))
    return pl.pallas_call(
        functools.partial(_route_kernel, n_experts=n_experts, tr=tr),
        grid=(t // tr,),
        in_specs=[pl.BlockSpec((tr, ROUTER_LANES), lambda i: (i, 0)),
                  pl.BlockSpec((n_experts, 1), lambda i: (0, 0))],
        out_specs=[out_blk, out_blk, out_blk, pl.BlockSpec((n_experts, 1), lambda i: (0, 0))],
        out_shape=[jax.ShapeDtypeStruct((TOPK, t), I32), jax.ShapeDtypeStruct((TOPK, t), F32),
                   jax.ShapeDtypeStruct((TOPK, t), I32), jax.ShapeDtypeStruct((n_experts, 1), F32)],
        scratch_shapes=[pltpu.VMEM((n_experts, 1), F32)],
        compiler_params=_cparams("arbitrary"),
    )(logits, e_bias_col)


def _row_copy(src, src_row, dst, dst_row, sem):
    return pltpu.make_async_copy(src.at[pl.ds(src_row, 1), :], dst.at[pl.ds(dst_row, 1), :], sem)


def _experts_kernel(be_ref, tokc_ref, tokn_ref, dstp_ref, dstc_ref, h_hbm, wg_ref, wu_ref, wd_ref, y_hbm,
                    xbuf0, xbuf1, ybuf0, ybuf1, gsem, ssem, *, tm):
    i = pl.program_id(0)
    nblk = be_ref.shape[0]

    def gather_done(xb, sem):
        pltpu.make_async_copy(h_hbm.at[pl.ds(0, tm), :], xb, sem).wait()

    def scatter_done():
        pltpu.make_async_copy(ybuf0, y_hbm.at[pl.ds(0, tm), :], ssem.at[0]).wait()

    @pl.when(i == 0)
    def _():
        ybuf1[...] = jnp.zeros(ybuf1.shape, ybuf1.dtype)

        def body(r, carry):
            _row_copy(h_hbm, tokc_ref[0, 0, r], xbuf0, r, gsem.at[0]).start()
            return carry
        lax.fori_loop(0, tm, body, 0)

    @pl.when(i >= 1)
    def _():
        scatter_done()

    def issue(x_nxt, y_prev, g_nxt):
        for r in range(tm):
            _row_copy(h_hbm, tokn_ref[0, 0, r], x_nxt, r, g_nxt).start(priority=r % 2)
            _row_copy(y_prev, r, y_hbm, dstp_ref[0, 0, r], ssem.at[0]).start(priority=(r + 1) % 2)

    def compute(x_cur, y_cur, g_cur):
        x = _unpack_rows_bf16(x_cur[...])
        g = jnp.dot(x, wg_ref[0, 0].astype(BF16), preferred_element_type=F32)
        u = jnp.dot(x, wu_ref[0, 0].astype(BF16), preferred_element_type=F32)
        hid = (g * _sigmoid(g) * u).astype(BF16)
        y = jnp.dot(hid, wd_ref[0, 0].astype(BF16), preferred_element_type=F32)
        half = y.shape[1] // 2
        y_cur[...] = _pack_bf16_pair(y[:, :half], y[:, half:])

    @pl.when(i % 2 == 0)
    def _():
        gather_done(xbuf0, gsem.at[0])
        issue(xbuf1, ybuf1, gsem.at[1])
        compute(xbuf0, ybuf0, gsem.at[0])

    @pl.when(i % 2 == 1)
    def _():
        issue(xbuf0, ybuf0, gsem.at[0])

    @pl.when(i % 2 == 1)
    def _():
        gather_done(xbuf1, gsem.at[1])
        compute(xbuf1, ybuf1, gsem.at[1])

    last = (nblk - 1) % 2
    x_extra, g_extra, y_last = (xbuf1, gsem.at[1], ybuf0) if last == 0 else (xbuf0, gsem.at[0], ybuf1)

    @pl.when(i == nblk - 1)
    def _():
        gather_done(x_extra, g_extra)
        scatter_done()

        def body(r, carry):
            _row_copy(y_last, r, y_hbm, dstc_ref[0, 0, r], ssem.at[0]).start()
            return carry
        lax.fori_loop(0, tm, body, 0)
        scatter_done()


def _experts(h2, slot_tok, slot_dst, blk_expert, w_gate, w_up, w_down, layer, tm, y_rows):
    d, de = w_gate.shape[2], w_gate.shape[3]
    dp = h2.shape[1]
    nblk = blk_expert.shape[0]
    tok3 = slot_tok.reshape(nblk, 1, tm)
    dst3 = slot_dst.reshape(nblk + 1, 1, tm)
    smem_blk = lambda f: pl.BlockSpec((1, 1, tm), f, memory_space=pltpu.SMEM)
    grid_spec = pltpu.PrefetchScalarGridSpec(
        num_scalar_prefetch=1,
        grid=(nblk,),
        in_specs=[smem_blk(lambda i, be: (i, 0, 0)),
                  smem_blk(lambda i, be: (jnp.minimum(i + 1, nblk - 1), 0, 0)),
                  smem_blk(lambda i, be: (i, 0, 0)),
                  smem_blk(lambda i, be: (i + 1, 0, 0)),
                  pl.BlockSpec(memory_space=pl.ANY),
                  pl.BlockSpec((1, 1, d, de), lambda i, be: (layer, be[i], 0, 0)),
                  pl.BlockSpec((1, 1, d, de), lambda i, be: (layer, be[i], 0, 0)),
                  pl.BlockSpec((1, 1, de, d), lambda i, be: (layer, be[i], 0, 0))],
        out_specs=pl.BlockSpec(memory_space=pl.ANY),
        scratch_shapes=[pltpu.VMEM((tm, dp), jnp.uint32)] * 4
        + [pltpu.SemaphoreType.DMA((2,)), pltpu.SemaphoreType.DMA((1,))],
    )
    return pl.pallas_call(
        functools.partial(_experts_kernel, tm=tm),
        grid_spec=grid_spec,
        out_shape=jax.ShapeDtypeStruct((y_rows, dp), jnp.uint32),
        compiler_params=_cparams("arbitrary"),
    )(blk_expert, tok3, tok3, dst3, dst3, h2, w_gate, w_up, w_down)


def _combine_kernel(*refs, bt, rows, alpha):
    y_refs = refs[:TOPK]
    x1_ref, h_ref, gw_ref, g2_ref, wg_ref, wu_ref, wd_ref, lg_ref, lb_ref, o_ref = refs[TOPK:]
    x = _unpack_rows_bf16(h_ref[...])
    g = jnp.dot(x, wg_ref[...], preferred_element_type=F32)
    u = jnp.dot(x, wu_ref[...], preferred_element_type=F32)
    hid = (g * _sigmoid(g) * u).astype(BF16)
    moe = jnp.dot(hid, wd_ref[...], preferred_element_type=F32)
    half = moe.shape[1] // 2
    moe_lo, moe_hi = moe[:, :half], moe[:, half:]
    for j in range(TOPK):
        y_lo, y_hi = _unpack_bf16_pair(y_refs[j][...])
        w = gw_ref[:, j:j + 1]
        moe_lo = moe_lo + y_lo * w
        moe_hi = moe_hi + y_hi * w
    moe = jnp.concatenate([moe_lo, moe_hi], axis=1)
    y = alpha * x1_ref[...] + _per_batch(lambda m, g2: (1.0 + g2) * m, moe, (g2_ref,), bt, rows)
    o_ref[...] = _layer_norm(y, lg_ref[...], lb_ref[...])


def _combine(y_rows, gw, x1, h2, gate2, w_gate, w_up, w_down, ln_g, ln_b, rows_per_batch, alpha):
    t, d = x1.shape
    dp = h2.shape[1]
    tt, bt, vec_idx = _batch_tiling(t, rows_per_batch, 128)
    nt = t // tt
    row = pl.BlockSpec((1, d), lambda i: (0, 0))
    tile = pl.BlockSpec((tt, d), lambda i: (i, 0))
    packed_tile = pl.BlockSpec((tt, dp), lambda i: (i, 0))
    full = lambda a: pl.BlockSpec(a.shape, lambda i: (0, 0))
    slab = lambda j: pl.BlockSpec((tt, dp), lambda i: (j * nt + i, 0))
    return pl.pallas_call(
        functools.partial(_combine_kernel, bt=bt, rows=rows_per_batch, alpha=alpha),
        grid=(nt,),
        in_specs=[slab(j) for j in range(TOPK)] + [
            tile, packed_tile, pl.BlockSpec((tt, TOPK), lambda i: (i, 0)),
            pl.BlockSpec((bt, 1, d), lambda i: (vec_idx(i), 0, 0)),
            full(w_gate), full(w_up), full(w_down), row, row],
        out_specs=tile,
        out_shape=jax.ShapeDtypeStruct((t, d), F32),
        compiler_params=_cparams("arbitrary"),
    )(*([y_rows] * TOPK), x1, h2, gw, gate2, w_gate, w_up, w_down, ln_g, ln_b)


def _dispatch_plan(eidx_t, rank_t, counts, tm):
    k, t = eidx_t.shape
    n_experts = counts.shape[0]
    nblk = (t * k + tm - 1) // tm + n_experts
    padded = (counts + tm - 1) // tm * tm
    pend = jnp.cumsum(padded)
    pstart = pend - padded
    experts = jnp.arange(n_experts, dtype=I32)
    start_of = jnp.sum(jnp.where(eidx_t[:, :, None] == experts, pstart, 0), axis=-1)
    pos = (start_of + rank_t).reshape(-1)
    flat = jnp.full((nblk * tm,), -1, I32).at[pos].set(jnp.arange(k * t, dtype=I32), unique_indices=True)
    real = flat >= 0
    slot_tok = jnp.where(real, flat % t, 0)
    dump = k * t + jnp.arange(nblk * tm, dtype=I32) % tm
    slot_dst = jnp.concatenate([dump[:tm], jnp.where(real, flat, dump)])
    blk_row = jnp.arange(nblk, dtype=I32) * tm
    blk_expert = jnp.minimum(jnp.sum((pend[None, :] <= blk_row[:, None]).astype(I32), axis=1), n_experts - 1)
    return slot_tok, slot_dst, blk_expert


def _layer(x, mods, layer, wts, rows_per_batch, alpha, mix_state):
    b, s, d = x.shape
    t = b * s
    sh1, sc1, g1, sh2, sc2, g2 = mods
    x2 = x.reshape(t, d)
    n_heads = wts['b_f'].shape[1]
    da = n_heads * HEAD_DIM
    dc = d - da
    assert da == dc, "column-block indexing of the projection assumes equal attention / conv widths"
    cb = wts['conv_w'].shape[1] - 1

    p, flog = _in_proj(x2, sh1, sc1, wts['w_in_main'][layer], wts['w_in_f'][layer], rows_per_batch)
    p3 = p.reshape(b, s, p.shape[1])
    k_new = p3[:, :, da:2 * da]
    v_new = p3[:, :, 2 * da:3 * da]
    b_f_row = wts['b_f'][layer][None, :]
    conv_args = (wts['conv_w'][layer], wts['conv_b'][layer][None, :], wts['conv_ln_g'][layer][None, :],
                 wts['conv_ln_b'][layer][None, :])

    if mix_state is None:
        logf, _, aug_q, aug_k = _forget_cumsum(flog.reshape(b, s, n_heads), b_f_row,
                                               jnp.zeros((b, 1, n_heads), F32), True)
        attn = _attention(p3, 0, p3, 1, p3, 2, aug_q, aug_k, 0)
        conv, tail = _conv_module(p3, 3, p3, 3, 4, False, *conv_args)
    else:
        ck_all, cv_all, clf, cconv = mix_state
        past = ck_all.shape[1]
        _, cum_c, _, aug_kc = _forget_cumsum(clf, b_f_row, jnp.zeros((b, 1, n_heads), F32), False)
        logf, _, aug_q, aug_kn = _forget_cumsum(flog.reshape(b, s, n_heads), b_f_row,
                                                cum_c[:, past - 1:past, :], True)
        attn = _attention(p3, 0, ck_all, 0, cv_all, 0, aug_q, aug_kc, past, kv_batch_off=layer * b,
                          tail=(p3, 1, p3, 2, aug_kn))
        hist = jnp.pad(cconv, ((0, 0), (CONV_HALO - cb, 0), (0, 0)))
        conv, tail = _conv_module(p3, 3, hist, 0, 0, True, *conv_args)

    x1, h, logits = _out_proj(attn.reshape(t, da), conv.reshape(t, dc), x2, g1, sh2, sc2,
                              wts['w_out_a'][layer], wts['w_out_c'][layer],
                              wts['ln1_g'][layer][None, :], wts['ln1_b'][layer][None, :],
                              wts['wr_hi'][layer], wts['wr_lo'][layer], rows_per_batch, alpha)

    eidx_t, gw_t, rank_t, counts = _route(logits, wts['e_bias'][layer][:, None])
    tm_e = _tile(t, 256)
    slot_tok, slot_dst, blk_expert = _dispatch_plan(eidx_t, rank_t, counts[:, 0].astype(I32), tm_e)
    y_rows = _experts(h, slot_tok, slot_dst, blk_expert, wts['w_e_gate'], wts['w_e_up'], wts['w_e_down'],
                      layer, tm_e, TOPK * t + tm_e)
    x_out = _combine(y_rows, gw_t.T, x1, h, g2, wts['w_s_gate'][layer], wts['w_s_up'][layer],
                     wts['w_s_down'][layer], wts['ln2_g'][layer][None, :], wts['ln2_b'][layer][None, :],
                     rows_per_batch, alpha)

    state = (k_new.reshape(b, s, n_heads, HEAD_DIM), v_new.reshape(b, s, n_heads, HEAD_DIM), logf,
             tail[:, CONV_HALO - cb:, :])
    return x_out.reshape(b, s, d), state


def kernel(x_prompt, x_sample, c_prompt, c_sample, cache_k, cache_v, cache_logf, state_conv, w_ada, b_ada, w_in, b_f, conv_w, conv_b, conv_ln_g, conv_ln_b, w_out, ln1_g, ln1_b, w_router, e_bias, w_e_gate, w_e_up, w_e_down, w_s_gate, w_s_up, w_s_down, ln2_g, ln2_b):
    depth, d, _ = w_in.shape
    n_heads = b_f.shape[1]
    da = n_heads * HEAD_DIM
    n_experts = w_router.shape[2]
    alpha = (2 * depth) ** 0.25
    bp = x_prompt.shape[0]

    fo = 3 * da
    wr_pad = jnp.pad(w_router, ((0, 0), (0, 0), (0, ROUTER_LANES - n_experts)))
    wr_hi = wr_pad.astype(BF16)
    wts = dict(
        w_in_main=jnp.concatenate([w_in[:, :, :fo], w_in[:, :, fo + n_heads:]], axis=2).astype(BF16),
        w_in_f=w_in[:, :, fo:fo + n_heads].astype(BF16),
        b_f=b_f, conv_w=conv_w, conv_b=conv_b, conv_ln_g=conv_ln_g, conv_ln_b=conv_ln_b,
        w_out_a=w_out[:, :da, :].astype(BF16), w_out_c=w_out[:, da:, :].astype(BF16),
        ln1_g=ln1_g, ln1_b=ln1_b, ln2_g=ln2_g, ln2_b=ln2_b,
        wr_hi=wr_hi, wr_lo=(wr_pad - wr_hi.astype(F32)).astype(BF16), e_bias=e_bias,
        w_e_gate=w_e_gate, w_e_up=w_e_up, w_e_down=w_e_down,
        w_s_gate=w_s_gate.astype(BF16), w_s_up=w_s_up.astype(BF16), w_s_down=w_s_down.astype(BF16),
    )

    mod_all = _adaln(jnp.concatenate([c_prompt, c_sample], axis=0), w_ada, b_ada)

    def mods(layer, lo, hi):
        m = mod_all[layer, lo:hi]
        return tuple(m[:, None, i * d:(i + 1) * d] for i in range(6))

    xp, xs = x_prompt, x_sample
    outs_p, outs_s = [], []
    for layer in range(depth):
        xp, st = _layer(xp, mods(layer, 0, bp), layer, wts, x_prompt.shape[1], alpha, None)
        outs_p.append(st)
        cache = (cache_k.reshape(-1, cache_k.shape[2], da), cache_v.reshape(-1, cache_v.shape[2], da),
                 cache_logf[layer], state_conv[layer])
        xs, st = _layer(xs, mods(layer, bp, bp + x_sample.shape[0]), layer, wts, x_sample.shape[1], alpha, cache)
        outs_s.append(st)

    stack = lambda outs, i: jnp.stack([o[i] for o in outs])
    return (xp, xs, stack(outs_p, 0), stack(outs_p, 1), stack(outs_p, 2), stack(outs_p, 3),
            stack(outs_s, 0), stack(outs_s, 1), stack(outs_s, 2), stack(outs_s, 3))
```
